```python
import math
import jax
import jax.numpy as jnp
from jax import lax
import numpy as np

D_MODEL = 1024
BATCH = 4
SEQ = 8192
DEPTH = 2

GRID_W = 64
CTX_LEN = 256
BRANCH_W = 512
N_BRANCH = 3
RWKV_HEADS = 8
RWKV_HEAD = BRANCH_W // RWKV_HEADS
LORA_W = 64
LORA_A = 64
LORA_G = 128
RWKV_COLS = 3 * BRANCH_W + LORA_W + LORA_A + LORA_G
RWKV_GN_EPS = 64e-5
NA_HEADS = 8
NA_HEAD = BRANCH_W // NA_HEADS
NA_KH = 8
NA_KW = 16
NA_COLS = 3 * BRANCH_W
S5_GROUP = 16
S5_GROUPS = BRANCH_W // S5_GROUP
S5_STATE = 64
GATE_COLS = N_BRANCH * D_MODEL
N_IN = RWKV_COLS + NA_COLS + BRANCH_W + GATE_COLS
N_EXPERTS = 32
TOP_K = 4
D_EXPERT = D_MODEL
SWIGLU_LIMIT = 7.0
SWIGLU_ALPHA = 1.702
MOE_BLOCK = 128
NORM_EPS = 1e-6

kernel_name = 'hybrid_rwkv7_natten_s5_moe_dit'


def rmsnorm(x, g):
    xf = x.astype(jnp.float32)
    y = xf * lax.rsqrt(jnp.mean(xf * xf, axis=-1, keepdims=True) + NORM_EPS)
    return (y * g.astype(jnp.float32)).astype(x.dtype)


def token_shift(y, mu_prev, mu_next):
    zero = jnp.zeros_like(y[:, :1])
    prev = jnp.concatenate([zero, y[:, :-1]], axis=1)
    nxt = jnp.concatenate([y[:, 1:], zero], axis=1)
    return y + mu_prev * (prev - y) + mu_next * (nxt - y)


def rwkv_prepare(p, lp):
    z = token_shift(p, lp['rwkv_mu_prev'], lp['rwkv_mu_next']).astype(jnp.float32)
    b, t = z.shape[:2]
    heads = lambda a: a.reshape(b, t, RWKV_HEADS, RWKV_HEAD)
    o = [0, BRANCH_W, 2 * BRANCH_W, 3 * BRANCH_W, 3 * BRANCH_W + LORA_W,
         3 * BRANCH_W + LORA_W + LORA_A, RWKV_COLS]
    r, k, v, wd, ad, gd = [z[..., o[j]:o[j + 1]] for j in range(6)]
    kk = heads(k * lp['rwkv_k_k'])
    kk = kk / jnp.maximum(jnp.sqrt(jnp.sum(kk * kk, -1, keepdims=True)), 1e-12)
    g = jax.nn.sigmoid(gd) @ lp['rwkv_g2']
    dirs = []
    for d in range(2):
        w = -jax.nn.softplus(-(lp['rwkv_w0'][d] + jnp.tanh(wd) @ lp['rwkv_w2'][d])) - 0.5
        a = jax.nn.sigmoid(lp['rwkv_a0'][d] + ad @ lp['rwkv_a2'][d])
        k_d = k * (1.0 + (a - 1.0) * lp['rwkv_k_a'])
        dirs.append((heads(jnp.exp(-jnp.exp(w))), heads(a), heads(k_d)))
    return heads(r), heads(v), kk, g, dirs


def rwkv_scan(prep, d, s0):
    r, v, kk, _, dirs = prep
    decay, a, k = dirs[d]

    def step(s, inp):
        r_t, w_t, kk_t, a_t, k_t, v_t = inp
        sa = jnp.einsum('bhvk,bhk->bhv', s, -kk_t)
        s = (s * w_t[:, :, None, :] + sa[..., None] * (kk_t * a_t)[:, :, None, :]
             + v_t[..., None] * k_t[:, :, None, :])
        return s, jnp.einsum('bhvk,bhk->bhv', s, r_t)

    xs = tuple(jnp.moveaxis(u, 1, 0) for u in (r, decay, kk, a, k, v))
    s_last, ys = lax.scan(step, s0, xs, reverse=(d == 1))
    return jnp.moveaxis(ys, 0, 1), s_last


def rwkv_readout(ys, prep, lp):
    r, v, _, g, dirs = prep
    b, t = r.shape[:2]
    y = ys[0] + ys[1]
    mu = jnp.mean(y, -1, keepdims=True)
    var = jnp.mean(jnp.square(y - mu), -1, keepdims=True)
    yn = ((y - mu) * lax.rsqrt(var + RWKV_GN_EPS)).reshape(b, t, BRANCH_W)
    bonus = sum(jnp.sum(r * dirs[d][2] * lp['rwkv_r_k'], -1, keepdims=True) for d in range(2)) * v
    return (yn * lp['rwkv_ln_w'] + lp['rwkv_ln_b'] + bonus.reshape(b, t, BRANCH_W)) * g


def rwkv_branch(pc, pl, lp, ctx_out):
    prep_c = rwkv_prepare(pc, lp)
    prep_l = rwkv_prepare(pl, lp)
    s0 = jnp.zeros((pl.shape[0], RWKV_HEADS, RWKV_HEAD, RWKV_HEAD), jnp.float32)
    ys_c, ys_l = [], []
    for d in range(2):
        y_c, s_ctx = rwkv_scan(prep_c, d, s0)
        y_l, _ = rwkv_scan(prep_l, d, s_ctx)
        ys_c.append(y_c)
        ys_l.append(y_l)
    out_c = rwkv_readout(ys_c, prep_c, lp) if ctx_out else None
    return out_c, rwkv_readout(ys_l, prep_l, lp)


def na_branch(pc, pl, rpb, ctx_out):
    b, s = pl.shape[:2]
    l = pc.shape[1]
    rows = s // GRID_W
    kh = min(NA_KH, rows)
    scale = NA_HEAD ** -0.5

    def split_heads(p):
        return [u.reshape(u.shape[0], u.shape[1], NA_HEADS, NA_HEAD) for u in jnp.split(p, 3, axis=-1)]

    qc, kc, vc = split_heads(pc)
    ql, kl, vl = split_heads(pl)
    kg = kl.reshape(b, rows, GRID_W, NA_HEADS, NA_HEAD)
    vg = vl.reshape(b, rows, GRID_W, NA_HEADS, NA_HEAD)
    q_rows = jnp.moveaxis(ql.reshape(b, rows, GRID_W, NA_HEADS, NA_HEAD), 1, 0)
    cols = np.arange(GRID_W)
    col_start = np.clip(cols - NA_KW // 2, 0, GRID_W - NA_KW)
    col_idx = col_start[:, None] + np.arange(NA_KW)[None, :]
    col_off = col_idx - cols[:, None] + NA_KW - 1

    def row_block(args):
        q_r, r = args
        rs = jnp.clip(r - kh // 2, 0, rows - kh)
        kb = lax.dynamic_slice_in_dim(kg, rs, kh, axis=1)[:, :, col_idx]
        vb = lax.dynamic_slice_in_dim(vg, rs, kh, axis=1)[:, :, col_idx]
        row_off = rs + jnp.arange(kh) - r + NA_KH - 1
        bias = rpb[:, row_off[None, :, None], col_off[:, None, :]]
        s_loc = jnp.einsum('bqhd,biqjhd->bhqij', q_r, kb) * scale + bias[None]
        s_ctx = jnp.einsum('bqhd,blhd->bhql', q_r, kc) * scale
        sc = jnp.concatenate([s_loc.reshape(b, NA_HEADS, GRID_W, kh * NA_KW), s_ctx], axis=-1)
        pr = jax.nn.softmax(sc.astype(jnp.float32), axis=-1)
        p_loc = pr[..., :kh * NA_KW].reshape(b, NA_HEADS, GRID_W, kh, NA_KW)
        p_ctx = pr[..., kh * NA_KW:]
        return (jnp.einsum('bhqij,biqjhd->bqhd', p_loc, vb)
                + jnp.einsum('bhql,blhd->bqhd', p_ctx, vc))

    o_rows = lax.map(row_block, (q_rows, jnp.arange(rows)))
    out_l = jnp.moveaxis(o_rows, 0, 1).reshape(b, s, BRANCH_W)
    out_c = None
    if ctx_out:
        p_c = jax.nn.softmax((jnp.einsum('bqhd,bkhd->bhqk', qc, kc) * scale).astype(jnp.float32), axis=-1)
        out_c = jnp.einsum('bhqk,bkhd->bqhd', p_c, vc).reshape(b, l, BRANCH_W)
    return out_c, out_l


def s5_discretize(lam_re, lam_im, log_dt, b_re, b_im):
    lam_re, lam_im, b_re, b_im = (u.astype(jnp.float32) for u in (lam_re, lam_im, b_re, b_im))
    dt = jnp.exp(log_dt.astype(jnp.float32))[:, None]
    mag = jnp.exp(lam_re * dt)
    ar = mag * jnp.cos(lam_im * dt)
    ai = mag * jnp.sin(lam_im * dt)
    den = lam_re * lam_re + lam_im * lam_im
    cr = ((ar - 1.0) * lam_re + ai * lam_im) / den
    ci = (ai * lam_re - (ar - 1.0) * lam_im) / den
    bbr = cr[..., None] * b_re - ci[..., None] * b_im
    bbi = cr[..., None] * b_im + ci[..., None] * b_re
    return ar, ai, bbr, bbi


def s5_combine(e1, e2):
    ar1, ai1, br1, bi1 = e1
    ar2, ai2, br2, bi2 = e2
    return (ar1 * ar2 - ai1 * ai2, ar1 * ai2 + ai1 * ar2,
            ar2 * br1 - ai2 * bi1 + br2, ar2 * bi1 + ai2 * br1 + bi2)


def s5_scan(u, disc, c_re, c_im, x0r, x0i, reverse):
    ar, ai, bbr, bbi = disc
    ut = jnp.moveaxis(u, 1, 0)
    if reverse:
        ut = ut[::-1]
    br = jnp.einsum('gpc,tbgc->tbgp', bbr, ut).at[0].add(ar * x0r - ai * x0i)
    bi = jnp.einsum('gpc,tbgc->tbgp', bbi, ut).at[0].add(ar * x0i + ai * x0r)
    shape = (ut.shape[0], 1) + ar.shape
    _, _, xr, xi = lax.associative_scan(
        s5_combine, (jnp.broadcast_to(ar, shape), jnp.broadcast_to(ai, shape), br, bi), axis=0)
    y = jnp.einsum('gcp,tbgp->tbgc', c_re, xr) - jnp.einsum('gcp,tbgp->tbgc', c_im, xi)
    if reverse:
        y = y[::-1]
    return jnp.moveaxis(y, 0, 1), xr[-1], xi[-1]


def s5_branch(uc, ul, lp, ctx_out):
    def groups(u):
        return u.astype(jnp.float32).reshape(u.shape[0], u.shape[1], S5_GROUPS, S5_GROUP)

    uc, ul = groups(uc), groups(ul)
    zero = jnp.zeros((ul.shape[0], S5_GROUPS, S5_STATE), jnp.float32)
    yc, yl = 0.0, 0.0
    for d in range(2):
        disc = s5_discretize(lp['s5_lambda_re'][d], lp['s5_lambda_im'][d], lp['s5_log_dt'][d],
                             lp['s5_b_re'][d], lp['s5_b_im'][d])
        y_c, xr, xi = s5_scan(uc, disc, lp['s5_c_re'][d], lp['s5_c_im'][d], zero, zero, d == 1)
        y_l, _, _ = s5_scan(ul, disc, lp['s5_c_re'][d], lp['s5_c_im'][d], xr, xi, d == 1)
        yc, yl = yc + y_c, yl + y_l

    def readout(y, u):
        bb, t = u.shape[:2]
        y = (y + u * lp['s5_d'].reshape(S5_GROUPS, S5_GROUP)).reshape(bb, t, BRANCH_W)
        y = jax.nn.gelu(y)
        return y * jax.nn.sigmoid(y @ lp['s5_glu_w'] + lp['s5_glu_b'])

    return (readout(yc, uc) if ctx_out else None), readout(yl, ul)


def mixer(hc, hl, lp, ctx_out):
    pc = hc @ lp['w_in']
    pl = hl @ lp['w_in']
    o1 = RWKV_COLS
    o2 = o1 + NA_COLS
    o3 = o2 + BRANCH_W
    a_c, a_l = rwkv_branch(pc[..., :o1], pl[..., :o1], lp, ctx_out)
    n_c, n_l = na_branch(pc[..., o1:o2], pl[..., o1:o2], lp['na_rpb'], ctx_out)
    s_c, s_l = s5_branch(pc[..., o2:o3], pl[..., o2:o3], lp, ctx_out)

    def merge(p, outs):
        y = 0.0
        for j in range(N_BRANCH):
            gate = jax.nn.sigmoid(p[..., o3 + j * D_MODEL:o3 + (j + 1) * D_MODEL])
            y = y + gate * (outs[j] @ lp['w_branch'][j])
        return y @ lp['w_out']

    out_c = merge(pc, (a_c, n_c, s_c)) if ctx_out else None
    return out_c, merge(pl, (a_l, n_l, s_l))


def moe(h, router_w, router_b, gu_w, gu_b, dn_w, dn_b):
    n, d = h.shape
    logits = (h @ router_w + router_b).astype(jnp.float32)
    top_val, top_idx = lax.top_k(logits, TOP_K)
    weights = jax.nn.softmax(top_val, axis=-1).reshape(-1)
    e_flat = top_idx.reshape(-1)
    n_assign = n * TOP_K
    order = jnp.argsort(e_flat)
    e_sorted = e_flat[order]
    counts = jnp.bincount(e_flat, length=N_EXPERTS)
    group_start = jnp.cumsum(counts) - counts
    padded = (counts + MOE_BLOCK - 1) // MOE_BLOCK * MOE_BLOCK
    pad_end = jnp.cumsum(padded)
    pad_start = pad_end - padded
    dest = pad_start[e_sorted] + jnp.arange(n_assign) - group_start[e_sorted]
    n_blocks = -(-n_assign // MOE_BLOCK) + N_EXPERTS
    n_slots = n_blocks * MOE_BLOCK
    slot_tok = jnp.full((n_slots,), n, jnp.int32).at[dest].set((order // TOP_K).astype(jnp.int32))
    slot_w = jnp.zeros((n_slots,), jnp.float32).at[dest].set(weights[order])
    block_expert = jnp.minimum(
        jnp.searchsorted(pad_end, jnp.arange(n_blocks) * MOE_BLOCK, side='right'), N_EXPERTS - 1)
    h_pad = jnp.concatenate([h, jnp.zeros((1, d), h.dtype)], axis=0)

    def expert_block(args):
        tok, e = args
        xb = h_pad[tok]
        gu = xb @ gu_w[e] + gu_b[e]
        glu, lin = jnp.split(gu, 2, axis=-1)
        glu = jnp.minimum(glu, SWIGLU_LIMIT)
        lin = jnp.clip(lin, -SWIGLU_LIMIT, SWIGLU_LIMIT)
        act = glu * jax.nn.sigmoid(SWIGLU_ALPHA * glu) * (lin + 1.0)
        return act @ dn_w[e] + dn_b[e]

    yb = lax.map(expert_block, (slot_tok.reshape(n_blocks, MOE_BLOCK), block_expert))
    yb = yb.reshape(n_slots, d)
    y = jnp.zeros((n + 1, d), yb.dtype).at[slot_tok].add(yb * slot_w[:, None].astype(yb.dtype))
    return y[:n]


def setup_inputs(seed: int = 0) -> dict:
    key = jax.random.key(seed)
    keys = iter(jax.random.split(key, 64))

    def nrm(shape, scale):
        return jax.random.normal(next(keys), shape, jnp.float32) * scale

    def uni(shape, lo, hi):
        return jax.random.uniform(next(keys), shape, jnp.float32, lo, hi)

    D, L, E, F = D_MODEL, DEPTH, N_EXPERTS, D_EXPERT
    G, P, C = S5_GROUPS, S5_STATE, S5_GROUP
    return {
        'x': nrm((BATCH, SEQ, D), 1.0),
        'c': nrm((BATCH, D), 1.0),
        'ctx': nrm((BATCH, CTX_LEN, D), 1.0),
        'c_ctx': nrm((D,), 1.0),
        'ada_w': nrm((L, D, 6 * D), 0.5 * D ** -0.5),
        'ada_b': nrm((L, 6 * D), 0.02),
        'norm1_g': 1.0 + nrm((L, D), 0.02),
        'norm2_g': 1.0 + nrm((L, D), 0.02),
        'w_in': nrm((L, D, N_IN), D ** -0.5),
        'rwkv_mu_prev': uni((L, RWKV_COLS), 0.0, 0.5),
        'rwkv_mu_next': uni((L, RWKV_COLS), 0.0, 0.5),
        'rwkv_w0': uni((L, 2, BRANCH_W), -6.0, -1.0),
        'rwkv_w2': nrm((L, 2, LORA_W, BRANCH_W), 0.1 * LORA_W ** -0.5),
        'rwkv_a0': nrm((L, 2, BRANCH_W), 0.1),
        'rwkv_a2': nrm((L, 2, LORA_A, BRANCH_W), 0.5 * LORA_A ** -0.5),
        'rwkv_g2': nrm((L, LORA_G, BRANCH_W), LORA_G ** -0.5),
        'rwkv_k_k': 0.85 + nrm((L, BRANCH_W), 0.02),
        'rwkv_k_a': 1.0 + nrm((L, BRANCH_W), 0.02),
        'rwkv_r_k': nrm((L, RWKV_HEADS, RWKV_HEAD), 0.1),
        'rwkv_ln_w': 1.0 + nrm((L, BRANCH_W), 0.02),
        'rwkv_ln_b': nrm((L, BRANCH_W), 0.02),
        'na_rpb': nrm((L, NA_HEADS, 2 * NA_KH - 1, 2 * NA_KW - 1), 0.1),
        's5_lambda_re': -0.5 + nrm((L, 2, G, P), 0.01),
        's5_lambda_im': jnp.pi * jnp.arange(P, dtype=jnp.float32) + nrm((L, 2, G, P), 0.01),
        's5_log_dt': uni((L, 2, G), math.log(1e-3), math.log(1e-1)),
        's5_b_re': nrm((L, 2, G, P, C), (2 * C) ** -0.5),
        's5_b_im': nrm((L, 2, G, P, C), (2 * C) ** -0.5),
        's5_c_re': nrm((L, 2, G, C, P), P ** -0.5),
        's5_c_im': nrm((L, 2, G, C, P), P ** -0.5),
        's5_d': nrm((L, BRANCH_W), 1.0),
        's5_glu_w': nrm((L, BRANCH_W, BRANCH_W), BRANCH_W ** -0.5),
        's5_glu_b': nrm((L, BRANCH_W), 0.02),
        'w_branch': nrm((L, N_BRANCH, BRANCH_W, D), BRANCH_W ** -0.5),
        'w_out': nrm((L, D, D), D ** -0.5),
        'router_w': nrm((L, D, E), D ** -0.5),
        'router_b': nrm((L, E), 0.01),
        'expert_gu_w': nrm((L, E, D, 2 * F), D ** -0.5),
        'expert_gu_b': nrm((L, E, 2 * F), 0.01),
        'expert_dn_w': nrm((L, E, F, D), F ** -0.5),
        'expert_dn_b': nrm((L, E, D), 0.01),
        'final_g': 1.0 + nrm((D,), 0.02),
    }


def reference(x, c, ctx, c_ctx, ada_w, ada_b, norm1_g, norm2_g, w_in, rwkv_mu_prev, rwkv_mu_next,
              rwkv_w0, rwkv_w2, rwkv_a0, rwkv_a2, rwkv_g2, rwkv_k_k, rwkv_k_a, rwkv_r_k, rwkv_ln_w,
              rwkv_ln_b, na_rpb, s5_lambda_re, s5_lambda_im, s5_log_dt, s5_b_re, s5_b_im, s5_c_re,
              s5_c_im, s5_d, s5_glu_w, s5_glu_b, w_branch, w_out, router_w, router_b, expert_gu_w,
              expert_gu_b, expert_dn_w, expert_dn_b, final_g):
    xl, xc = x, ctx
    c_act = jax.nn.silu(c)
    c_ctx_act = jax.nn.silu(c_ctx)
    for i in range(DEPTH):
        last = i == DEPTH - 1
        lp = {
            'w_in': w_in[i], 'rwkv_mu_prev': rwkv_mu_prev[i], 'rwkv_mu_next': rwkv_mu_next[i],
            'rwkv_w0': rwkv_w0[i], 'rwkv_w2': rwkv_w2[i], 'rwkv_a0': rwkv_a0[i], 'rwkv_a2': rwkv_a2[i],
            'rwkv_g2': rwkv_g2[i], 'rwkv_k_k': rwkv_k_k[i], 'rwkv_k_a': rwkv_k_a[i],
            'rwkv_r_k': rwkv_r_k[i], 'rwkv_ln_w': rwkv_ln_w[i], 'rwkv_ln_b': rwkv_ln_b[i],
            'na_rpb': na_rpb[i],
            's5_lambda_re': s5_lambda_re[i], 's5_lambda_im': s5_lambda_im[i], 's5_log_dt': s5_log_dt[i],
            's5_b_re': s5_b_re[i], 's5_b_im': s5_b_im[i], 's5_c_re': s5_c_re[i], 's5_c_im': s5_c_im[i],
            's5_d': s5_d[i], 's5_glu_w': s5_glu_w[i], 's5_glu_b': s5_glu_b[i],
            'w_branch': w_branch[i], 'w_out': w_out[i],
        }
        m_l = jnp.split((c_act @ ada_w[i] + ada_b[i])[:, None, :], 6, axis=-1)
        m_c = jnp.split(c_ctx_act @ ada_w[i] + ada_b[i], 6, axis=-1)
        hl = rmsnorm(xl, norm1_g[i]) * (1.0 + m_l[1]) + m_l[0]
        hc = rmsnorm(xc, norm1_g[i]) * (1.0 + m_c[1]) + m_c[0]
        oc, ol = mixer(hc, hl, lp, not last)
        xl = xl + m_l[2] * ol
        hl = rmsnorm(xl, norm2_g[i]) * (1.0 + m_l[4]) + m_l[3]
        moe_args = (router_w[i], router_b[i], expert_gu_w[i], expert_gu_b[i], expert_dn_w[i], expert_dn_b[i])
        if last:
            b, s, d = hl.shape
            xl = xl + m_l[5] * moe(hl.reshape(b * s, d), *moe_args).reshape(b, s, d)
        else:
            xc = xc + m_c[2] * oc
            hc = rmsnorm(xc, norm2_g[i]) * (1.0 + m_c[4]) + m_c[3]
            b, l, d = hc.shape
            s = hl.shape[1]
            y = moe(jnp.concatenate([hc, hl], axis=1).reshape(b * (l + s), d), *moe_args).reshape(b, l + s, d)
            xc = xc + m_c[5] * y[:, :l]
            xl = xl + m_l[5] * y[:, l:]
    return rmsnorm(xl, final_g)
```

```python
import functools
import math

import numpy as np
import jax
import jax.numpy as jnp
from jax import lax
from jax.experimental import pallas as pl
from jax.experimental.pallas import tpu as pltpu

f32 = jnp.float32
bf16 = jnp.bfloat16

D_MODEL = 1024
BRANCH_W = 512
GRID_W = 64
HEADS = 8
HEAD = 64
LORA_W = 64
LORA_A = 64
LORA_G = 128
RWKV_COLS = 3 * BRANCH_W + LORA_W + LORA_A + LORA_G
RWKV_GN_EPS = 64e-5
NA_KH = 8
NA_KW = 16
S5_GROUP = 16
S5_GROUPS = BRANCH_W // S5_GROUP
S5_STATE = 64
N_EXPERTS = 32
TOP_K = 4
SWIGLU_LIMIT = 7.0
SWIGLU_ALPHA = 1.702
NORM_EPS = 1e-6

TM = 256
LANES = 128
COL_PAD = 256
P_COLS = RWKV_COLS + COL_PAD + 3 * BRANCH_W + BRANCH_W + 3 * D_MODEL
CB_Q, CB_K, CB_V, CB_U = 4, 5, 6, 7
CB_GATE = 4
IN_TN = 1792
NA_NEG = -1e30
S5_CB = 128
S5_NQ = BRANCH_W // S5_CB
S5_BS = (S5_CB // S5_GROUP) * S5_STATE
MOE_BM = 256
VMEM_LIMIT = 56 * 1024 * 1024


def _cparams(sem, **kw):
    return pltpu.CompilerParams(dimension_semantics=sem, vmem_limit_bytes=VMEM_LIMIT, **kw)


def _segsum(x, e):
    hi = x.astype(bf16)
    lo = (x - hi.astype(f32)).astype(bf16)
    return jnp.dot(hi, e, preferred_element_type=f32) + jnp.dot(lo, e, preferred_element_type=f32)


def _sigmoid(x):
    return 1.0 / (1.0 + jnp.exp(-x))


def _softplus(x):
    return jnp.maximum(x, 0.0) + jnp.log(1.0 + jnp.exp(-jnp.abs(x)))


def _rms_mod(x, g, scale, shift):
    y = x * lax.rsqrt(jnp.mean(x * x, axis=-1, keepdims=True) + NORM_EPS)
    return y * g * (1.0 + scale) + shift


def _ada_kernel(c_ref, w_ref, b_ref, o_ref):
    c = c_ref[...]
    act = (c * _sigmoid(c)).astype(bf16)
    o_ref[0] = jnp.dot(act, w_ref[0].astype(bf16), preferred_element_type=f32) + b_ref[0]


def _ada(c8, ada_w, ada_b):
    depth, d, n = ada_w.shape
    tn = 1536
    return pl.pallas_call(
        _ada_kernel,
        grid=(depth, n // tn),
        in_specs=[pl.BlockSpec((8, d), lambda l, j: (0, 0)),
                  pl.BlockSpec((1, d, tn), lambda l, j: (l, 0, j)),
                  pl.BlockSpec((1, 1, tn), lambda l, j: (l, 0, j))],
        out_specs=pl.BlockSpec((1, 8, tn), lambda l, j: (l, 0, j)),
        out_shape=jax.ShapeDtypeStruct((depth, 8, n), f32),
        compiler_params=_cparams(("parallel", "parallel")),
        name="ada_mod",
    )(c8, ada_w, ada_b.reshape(depth, 1, n))


def _in_kernel(x_ref, g_ref, mod_ref, w_ref, o_ref, h_s):
    @pl.when(pl.program_id(1) == 0)
    def _():
        mod = mod_ref[0]
        h_s[...] = _rms_mod(x_ref[...], g_ref[...], mod[1:2], mod[0:1]).astype(bf16)

    o_ref[...] = jnp.dot(h_s[...], w_ref[...], preferred_element_type=f32)


def _seg_of_tile(i, tpb):
    return 2 * (i // tpb) + jnp.minimum(i % tpb, 1)


def _proj_in(x2, g, modt, w7, tpb):
    m, d = x2.shape
    return pl.pallas_call(
        _in_kernel,
        grid=(m // TM, P_COLS // IN_TN),
        in_specs=[pl.BlockSpec((TM, d), lambda i, j: (i, 0)),
                  pl.BlockSpec((1, d), lambda i, j: (0, 0)),
                  pl.BlockSpec((1, 8, d), lambda i, j: (_seg_of_tile(i, tpb), 0, 0)),
                  pl.BlockSpec((d, IN_TN), lambda i, j: (0, j))],
        out_specs=pl.BlockSpec((TM, IN_TN), lambda i, j: (i, j)),
        out_shape=jax.ShapeDtypeStruct((m, P_COLS), f32),
        scratch_shapes=[pltpu.VMEM((TM, d), bf16)],
        compiler_params=_cparams(("parallel", "arbitrary")),
        name="proj_in",
    )(x2, g.reshape(1, d), modt, w7)


def _rwkv_prep_kernel(p_ref, pp_ref, pn_ref, mu_ref, vec_ref, w2_ref, a2_ref, g2_ref, e_ref,
                      r_o, v_o, kk_o, d0_o, d1_o, b0_o, b1_o, k0_o, k1_o, g_o, bonus_o, *, tpb):
    j = pl.program_id(0) % tpb
    prev_ok = jnp.logical_and(j != 0, j != 1)
    next_ok = jnp.logical_and(j != 0, j != tpb - 1)
    p = p_ref[...]
    row = lax.broadcasted_iota(jnp.int32, p.shape, 0)
    hp = jnp.where(prev_ok, pp_ref[7:8, :], 0.0)
    hn = jnp.where(next_ok, pn_ref[0:1, :], 0.0)
    prev = jnp.where(row == 0, hp, pltpu.roll(p, 1, 0))
    nxt = jnp.where(row == TM - 1, hn, pltpu.roll(p, TM - 1, 0))
    mu = mu_ref[...]
    z = p + mu[0:1] * (prev - p) + mu[1:2] * (nxt - p)

    r = z[:, 0:BRANCH_W]
    k = z[:, BRANCH_W:2 * BRANCH_W]
    v = z[:, 2 * BRANCH_W:3 * BRANCH_W]
    wa = z[:, 3 * BRANCH_W:3 * BRANCH_W + LORA_W + LORA_A]
    gd = z[:, 3 * BRANCH_W + LORA_W + LORA_A:RWKV_COLS]
    vec = vec_ref[...]
    e = e_ref[...]
    kk = k * vec[0:1]
    kk = kk / jnp.maximum(jnp.sqrt(_segsum(kk * kk, e)), 1e-12)
    g_o[...] = jnp.dot(_sigmoid(gd).astype(bf16), g2_ref[...], preferred_element_type=f32)
    tw = jnp.tanh(wa).astype(bf16)
    wab = wa.astype(bf16)
    ksum = None
    for d, (dec_o, b_o, k_o) in enumerate(((d0_o, b0_o, k0_o), (d1_o, b1_o, k1_o))):
        wl = vec[3 + d:4 + d] + jnp.dot(tw, w2_ref[d], preferred_element_type=f32)
        w = -_softplus(-wl) - 0.5
        dec_o[...] = jnp.exp(-jnp.exp(w))
        a = _sigmoid(vec[5 + d:6 + d] + jnp.dot(wab, a2_ref[d], preferred_element_type=f32))
        kd = k * (1.0 + (a - 1.0) * vec[1:2])
        b_o[...] = kk * a
        k_o[...] = kd
        ksum = kd if ksum is None else ksum + kd
    r_o[...] = r
    v_o[...] = v
    kk_o[...] = kk
    bonus_o[...] = _segsum(r * ksum * vec[2:3], e) * v


def _rwkv_prep(p, mu, vec, w2p, a2p, g2, e, tpb):
    m = p.shape[0]
    bw = BRANCH_W
    full = lambda shape: pl.BlockSpec(shape, lambda i: (0,) * len(shape))
    out = pl.BlockSpec((TM, bw), lambda i: (i, 0))
    return pl.pallas_call(
        functools.partial(_rwkv_prep_kernel, tpb=tpb),
        grid=(m // TM,),
        in_specs=[pl.BlockSpec((TM, RWKV_COLS), lambda i: (i, 0)),
                  pl.BlockSpec((8, RWKV_COLS), lambda i: (jnp.maximum(i * (TM // 8) - 1, 0), 0)),
                  pl.BlockSpec((8, RWKV_COLS), lambda i: (jnp.minimum((i + 1) * (TM // 8), m // 8 - 1), 0)),
                  full((2, RWKV_COLS)), full((8, bw)), full((2, LANES, bw)), full((2, LANES, bw)),
                  full((LORA_G, bw)), full((bw, bw))],
        out_specs=[out] * 11,
        out_shape=[jax.ShapeDtypeStruct((m, bw), f32)] * 11,
        compiler_params=_cparams(("parallel",)),
        name="rwkv_prep",
    )(p, p, p, mu, vec, w2p, a2p, g2, e)


def _rwkv_scan_kernel(w_ref, b_ref, k_ref, kkn_ref, r_ref, v_ref, y_ref, s_ref, sa_ref, *, tc):
    @pl.when(pl.program_id(0) == 0)
    def _():
        s_ref[...] = jnp.zeros_like(s_ref)
        sa_ref[...] = jnp.zeros_like(sa_ref)

    shape = sa_ref.shape

    def step(t, sa):
        v = v_ref[t]
        y = [jnp.zeros(shape, f32), jnp.zeros(shape, f32)]
        nsa = [jnp.zeros(shape, f32), jnp.zeros(shape, f32)]
        for k in range(HEAD):
            row = lambda ref: jnp.broadcast_to(ref[t, k:k + 1, :], shape)
            s_new = row(w_ref) * s_ref[k] - row(b_ref) * sa + row(k_ref) * v
            s_ref[k] = s_new
            y[k % 2] = y[k % 2] + row(r_ref) * s_new
            nsa[k % 2] = nsa[k % 2] + row(kkn_ref) * s_new
        y_ref[t] = y[0] + y[1]
        return nsa[0] + nsa[1]

    sa_ref[...] = lax.fori_loop(0, tc, step, sa_ref[...])


def _rwkv_scan(w, b, k, kkn, r, v, tc=32):
    ttot, _, lanes = w.shape
    vlo = v.shape[1]
    rows = pl.BlockSpec((tc, HEAD, lanes), lambda i: (i, 0, 0))
    tile = pl.BlockSpec((tc, vlo, lanes), lambda i: (i, 0, 0))
    return pl.pallas_call(
        functools.partial(_rwkv_scan_kernel, tc=tc),
        grid=(ttot // tc,),
        in_specs=[rows] * 5 + [tile],
        out_specs=tile,
        out_shape=jax.ShapeDtypeStruct((ttot, vlo, lanes), f32),
        scratch_shapes=[pltpu.VMEM((HEAD, vlo, lanes), f32), pltpu.VMEM((vlo, lanes), f32)],
        compiler_params=_cparams(("arbitrary",)),
        name="rwkv_scan",
    )(w, b, k, kkn, r, v)


def _to_scan_rows(a, ridx, b):
    af, ab = a
    t = af.shape[0] // b
    af = af.reshape(b, t, HEADS, HEAD)
    ab = ab.reshape(b, t, HEADS, HEAD)[:, ridx]
    x = jnp.stack([af, ab], 0)
    x = jnp.transpose(x, (2, 4, 0, 1, 3)).reshape(t, HEAD, 2 * b * HEADS)
    return jnp.concatenate([x, x], axis=-1)


def _to_scan_v(a, ridx, b):
    t = a.shape[0] // b
    af = a.reshape(b, t, HEADS, 2, HEAD // 2)
    x = jnp.stack([af, af[:, ridx]], 0)
    return jnp.transpose(x, (2, 5, 4, 0, 1, 3)).reshape(t, HEAD // 2, 4 * b * HEADS)


def _from_scan_y(y, ridx, b):
    t = y.shape[0]
    x = y.reshape(t, HEAD // 2, 2, 2, b, HEADS)
    x = jnp.transpose(x, (3, 4, 0, 5, 2, 1)).reshape(2, b, t, BRANCH_W)
    return (x[0] + x[1][:, ridx]).reshape(b * t, BRANCH_W)


def _softmax_pv(q, ks, vs, biases):
    ss = []
    for kt, bias in zip(ks, biases):
        s = lax.dot_general(q, kt, (((1,), (1,)), ((), ())), preferred_element_type=f32)
        ss.append(s if bias is None else s + bias)
    m = ss[0].max(axis=-1, keepdims=True)
    for s in ss[1:]:
        m = jnp.maximum(m, s.max(axis=-1, keepdims=True))
    den = None
    acc = None
    for s, vt in zip(ss, vs):
        pr = jnp.exp(s - m)
        d = pr.sum(axis=-1, keepdims=True)
        o = jnp.dot(pr.astype(bf16), vt, preferred_element_type=f32)
        den = d if den is None else den + d
        acc = o if acc is None else acc + o
    return acc / den


def _na_kernel(q_ref, kc_ref, vc_ref, k0_ref, k1_ref, k2_ref, v0_ref, v1_ref, v2_ref, bm_ref, o_ref):
    j = pl.program_id(1)
    scale = HEAD ** -0.5

    @pl.when(j == 0)
    def _():
        for h in range(HEADS):
            sl = slice(HEAD * h, HEAD * (h + 1))
            q = (q_ref[:, sl] * scale).astype(bf16)
            o_ref[:, sl] = _softmax_pv(q, [kc_ref[:, sl].astype(bf16)], [vc_ref[:, sl].astype(bf16)], [None])

    @pl.when(j > 0)
    def _():
        for h in range(HEADS):
            sl = slice(HEAD * h, HEAD * (h + 1))
            q = (q_ref[:, sl] * scale).astype(bf16)
            ks = [r[:, sl].astype(bf16) for r in (k0_ref, k1_ref, k2_ref, kc_ref)]
            vs = [r[:, sl].astype(bf16) for r in (v0_ref, v1_ref, v2_ref, vc_ref)]
            biases = [bm_ref[0, h, :, TM * s:TM * (s + 1)] for s in range(3)] + [None]
            o_ref[:, sl] = _softmax_pv(q, ks, vs, biases)


def _na(p, biasmask, b, tpb):
    m = p.shape[0]
    nlb = tpb - 1
    bw = BRANCH_W

    def kv(col, slot):
        def imap(bi, j):
            jj = jnp.maximum(j - 1, 0)
            return (bi * tpb + 1 + jnp.clip(jj - 1 + slot, 0, nlb - 1), col)
        return pl.BlockSpec((TM, bw), imap)

    def variant(bi, j):
        jj = jnp.maximum(j - 1, 0)
        return (jnp.where(jj == 0, 0, jnp.where(jj == nlb - 1, 2, 1)), 0, 0, 0)

    return pl.pallas_call(
        _na_kernel,
        grid=(b, tpb),
        in_specs=[pl.BlockSpec((TM, bw), lambda bi, j: (bi * tpb + j, CB_Q)),
                  pl.BlockSpec((TM, bw), lambda bi, j: (bi * tpb, CB_K)),
                  pl.BlockSpec((TM, bw), lambda bi, j: (bi * tpb, CB_V)),
                  kv(CB_K, 0), kv(CB_K, 1), kv(CB_K, 2), kv(CB_V, 0), kv(CB_V, 1), kv(CB_V, 2),
                  pl.BlockSpec((1, HEADS, TM, 3 * TM), variant)],
        out_specs=pl.BlockSpec((TM, bw), lambda bi, j: (bi * tpb + j, 0)),
        out_shape=jax.ShapeDtypeStruct((m, bw), f32),
        compiler_params=_cparams(("parallel", "arbitrary")),
        name="na_attn",
    )(p, p, p, p, p, p, p, p, p, biasmask)


def _na_bias_tables(rpb):
    qr = TM // GRID_W
    dr = np.arange(qr)
    krel = np.arange(3 * qr)
    qc = np.arange(GRID_W)
    kc = np.arange(GRID_W)
    roff = krel[None, :] - dr[:, None] + (NA_KH - 1 - qr)
    coff = kc[None, :] - qc[:, None] + NA_KW - 1
    cs = np.clip(qc - NA_KW // 2, 0, GRID_W - NA_KW)
    colvalid = (kc[None, :] >= cs[:, None]) & (kc[None, :] < cs[:, None] + NA_KW)
    rv_int = (krel[None, :] >= dr[:, None]) & (krel[None, :] < dr[:, None] + NA_KH)
    rv_first = np.broadcast_to((krel >= qr) & (krel < qr + NA_KH), rv_int.shape)
    rv_last = np.broadcast_to(krel < NA_KH, rv_int.shape)
    rowvalid = np.stack([rv_first, rv_int, rv_last], 0)
    valid = rowvalid[:, :, None, :, None] & colvalid[None, None, :, None, :]
    ri = np.clip(roff, 0, 2 * NA_KH - 2)
    ci = np.clip(coff, 0, 2 * NA_KW - 2)
    bias = rpb[:, ri[:, None, :, None], ci[None, :, None, :]]
    tab = jnp.where(valid[:, None], bias[None], NA_NEG)
    return tab.reshape(3, HEADS, TM, 3 * TM).astype(f32)


def _s5_kernel(u_ref, wb_ref, wc_ref, a_ref, y_ref, x_s, st_s, *, nb, reverse):
    rows = x_s.shape[0]
    ntile = rows // 8
    nsub = 8 // nb
    ncc = S5_BS // LANES
    shift = (8 - nb) if reverse else nb

    @pl.when(pl.program_id(0) == 0)
    def _():
        st_s[...] = jnp.zeros_like(st_s)

    grp = lax.broadcasted_iota(jnp.int32, (8, LANES), 0) // nb
    for q in range(S5_NQ):
        u = u_ref[:, S5_CB * q:S5_CB * (q + 1)].astype(bf16)
        x_s[...] = jnp.dot(u, wb_ref[q], preferred_element_type=f32)
        ar = [jnp.broadcast_to(a_ref[q, 0:1, LANES * c:LANES * (c + 1)], (8, LANES)) for c in range(ncc)]
        ai = [jnp.broadcast_to(a_ref[q, 1:2, LANES * c:LANES * (c + 1)], (8, LANES)) for c in range(ncc)]

        def tile_step(i, carry):
            rt = (ntile - 1 - i) if reverse else i
            r0 = pl.multiple_of(rt * 8, 8)
            new = []
            for c in range(ncc):
                cr, ci = carry[2 * c], carry[2 * c + 1]
                re_sl = slice(LANES * c, LANES * (c + 1))
                im_sl = slice(S5_BS + LANES * c, S5_BS + LANES * (c + 1))
                br = x_s[pl.ds(r0, 8), re_sl]
                bi = x_s[pl.ds(r0, 8), im_sl]
                out_r = out_i = None
                order = range(nsub - 1, -1, -1) if reverse else range(nsub)
                for s in order:
                    pr = pltpu.roll(cr, shift, 0)
                    pi = pltpu.roll(ci, shift, 0)
                    cr = ar[c] * pr - ai[c] * pi + br
                    ci = ar[c] * pi + ai[c] * pr + bi
                    out_r = cr if out_r is None else jnp.where(grp == s, cr, out_r)
                    out_i = ci if out_i is None else jnp.where(grp == s, ci, out_i)
                x_s[pl.ds(r0, 8), re_sl] = out_r
                x_s[pl.ds(r0, 8), im_sl] = out_i
                new += [out_r, out_i]
            return tuple(new)

        init = tuple(st_s[q, cc] for cc in range(2 * ncc))
        fin = lax.fori_loop(0, ntile, tile_step, init)
        for cc in range(2 * ncc):
            st_s[q, cc] = fin[cc]
        y_ref[:, S5_CB * q:S5_CB * (q + 1)] = jnp.dot(x_s[...].astype(bf16), wc_ref[q],
                                                      preferred_element_type=f32)


def _s5_scan(u_tb, wb, wc, a, nb, tpb, reverse):
    rows = TM * nb
    if reverse:
        imap = lambda c: (jnp.where(c == 0, 0, tpb - c), 0)
    else:
        imap = lambda c: (c, 0)
    full = lambda shape: pl.BlockSpec(shape, lambda c: (0,) * len(shape))
    return pl.pallas_call(
        functools.partial(_s5_kernel, nb=nb, reverse=reverse),
        grid=(tpb,),
        in_specs=[pl.BlockSpec((rows, BRANCH_W), imap),
                  full((S5_NQ, S5_CB, 2 * S5_BS)), full((S5_NQ, 2 * S5_BS, S5_CB)), full((S5_NQ, 2, S5_BS))],
        out_specs=pl.BlockSpec((rows, BRANCH_W), imap),
        out_shape=jax.ShapeDtypeStruct(u_tb.shape, f32),
        scratch_shapes=[pltpu.VMEM((rows, 2 * S5_BS), f32),
                        pltpu.VMEM((S5_NQ, 2 * S5_BS // LANES, 8, LANES), f32)],
        compiler_params=_cparams(("arbitrary",)),
        name="s5_bwd" if reverse else "s5_fwd",
    )(u_tb, wb, wc, a)


def _s5_params(lam_re, lam_im, log_dt, b_re, b_im, c_re, c_im):
    dt = jnp.exp(log_dt)[:, None]
    mag = jnp.exp(lam_re * dt)
    ar = mag * jnp.cos(lam_im * dt)
    ai = mag * jnp.sin(lam_im * dt)
    den = lam_re * lam_re + lam_im * lam_im
    cr = ((ar - 1.0) * lam_re + ai * lam_im) / den
    ci = (ai * lam_re - (ar - 1.0) * lam_im) / den
    bbr = cr[..., None] * b_re - ci[..., None] * b_im
    bbi = cr[..., None] * b_im + ci[..., None] * b_re
    gl = S5_CB // S5_GROUP
    eye = jnp.eye(gl, dtype=f32)

    def wb_half(bb):
        x = bb.reshape(S5_NQ, gl, S5_STATE, S5_GROUP)
        x = jnp.einsum('qgpc,gh->qgchp', x, eye)
        return x.reshape(S5_NQ, S5_CB, S5_BS)

    def wc_half(cc):
        x = cc.reshape(S5_NQ, gl, S5_GROUP, S5_STATE)
        x = jnp.einsum('qgcp,gh->qhpgc', x, eye)
        return x.reshape(S5_NQ, S5_BS, S5_CB)

    wb = jnp.concatenate([wb_half(bbr), wb_half(bbi)], axis=2).astype(bf16)
    wc = jnp.concatenate([wc_half(c_re), wc_half(-c_im)], axis=1).astype(bf16)
    a = jnp.stack([ar.reshape(S5_NQ, S5_BS), ai.reshape(S5_NQ, S5_BS)], axis=1)
    return wb, wc, a


def _gelu_tanh(x):
    return 0.5 * x * (1.0 + jnp.tanh(math.sqrt(2.0 / math.pi) * (x + 0.044715 * x * x * x)))


def _merge_kernel(x_ref, mod_ref, g2n_ref, yr_ref, bonus_ref, gg_ref, na_ref, y5f_ref, y5b_ref, u_ref,
                  ga_ref, gn_ref, gs_ref, lnv_ref, e_ref, gluw_ref, wbr_ref, wout_ref, rw_ref, rb_ref,
                  xo_ref, h2_ref, idx_ref, tw_ref):
    mod = mod_ref[0]
    lnv = lnv_ref[...]
    e = e_ref[...]
    y = yr_ref[...]
    mu = _segsum(y, e) * (1.0 / HEAD)
    dlt = y - mu
    var = _segsum(dlt * dlt, e) * (1.0 / HEAD)
    a_out = (dlt * lax.rsqrt(var + RWKV_GN_EPS) * lnv[0:1] + lnv[1:2] + bonus_ref[...]) * gg_ref[...]
    ys = _gelu_tanh(y5f_ref[...] + y5b_ref[...] + u_ref[...] * lnv[2:3])
    glu = jnp.dot(ys.astype(bf16), gluw_ref[...], preferred_element_type=f32) + lnv[3:4]
    s_out = ys * _sigmoid(glu)
    mix = None
    for o, gate_ref, jdx in ((a_out, ga_ref, 0), (na_ref[...], gn_ref, 1), (s_out, gs_ref, 2)):
        t = _sigmoid(gate_ref[...]) * jnp.dot(o.astype(bf16), wbr_ref[jdx], preferred_element_type=f32)
        mix = t if mix is None else mix + t
    ol = jnp.dot(mix.astype(bf16), wout_ref[...], preferred_element_type=f32)
    xn = x_ref[...] + mod[2:3] * ol
    xo_ref[...] = xn
    h2 = _rms_mod(xn, g2n_ref[...], mod[4:5], mod[3:4])
    h2_ref[...] = h2
    hi = h2.astype(bf16)
    lo = (h2 - hi.astype(f32)).astype(bf16)
    logits = (jnp.dot(hi, rw_ref[0], preferred_element_type=f32) + jnp.dot(lo, rw_ref[0], preferred_element_type=f32)
              + jnp.dot(hi, rw_ref[1], preferred_element_type=f32) + rb_ref[...])
    lane = lax.broadcasted_iota(jnp.int32, logits.shape, 1)
    logits = jnp.where(lane < N_EXPERTS, logits, -jnp.inf)
    idx_acc = jnp.zeros(logits.shape, jnp.int32)
    val_acc = jnp.full(logits.shape, -jnp.inf, f32)
    top = None
    for j in range(TOP_K):
        mx = logits.max(axis=-1, keepdims=True)
        sel = jnp.min(jnp.where(logits == mx, lane, LANES), axis=-1, keepdims=True)
        idx_acc = jnp.where(lane == j, sel, idx_acc)
        val_acc = jnp.where(lane == j, mx, val_acc)
        logits = jnp.where(lane == sel, -jnp.inf, logits)
        top = mx if top is None else top
    ex = jnp.exp(val_acc - top)
    idx_ref[...] = idx_acc
    tw_ref[...] = ex / ex.sum(axis=-1, keepdims=True)


def _merge(x2, modt, g2n, yr, bonus, gg, na, y5f, y5b, p, lnv, e, gluw, wbr, wout, rw, rb, tpb):
    m, d = x2.shape
    bw = BRANCH_W
    tok = lambda w: pl.BlockSpec((TM, w), lambda i: (i, 0))
    full = lambda shape: pl.BlockSpec(shape, lambda i: (0,) * len(shape))
    gate = lambda jdx: pl.BlockSpec((TM, d), lambda i: (i, CB_GATE + jdx))
    return pl.pallas_call(
        _merge_kernel,
        grid=(m // TM,),
        in_specs=[tok(d), pl.BlockSpec((1, 8, d), lambda i: (_seg_of_tile(i, tpb), 0, 0)), full((1, d)),
                  tok(bw), tok(bw), tok(bw), tok(bw), tok(bw), tok(bw),
                  pl.BlockSpec((TM, bw), lambda i: (i, CB_U)), gate(0), gate(1), gate(2),
                  full((8, bw)), full((bw, bw)), full((bw, bw)), full((3, bw, d)), full((d, d)),
                  full((2, d, LANES)), full((1, LANES))],
        out_specs=[tok(d), tok(d), tok(LANES), tok(LANES)],
        out_shape=[jax.ShapeDtypeStruct((m, d), f32), jax.ShapeDtypeStruct((m, d), f32),
                   jax.ShapeDtypeStruct((m, LANES), jnp.int32), jax.ShapeDtypeStruct((m, LANES), f32)],
        compiler_params=_cparams(("parallel",)),
        name="merge_router",
    )(x2, modt, g2n.reshape(1, d), yr, bonus, gg, na, y5f, y5b, p, p, p, p, lnv, e, gluw, wbr, wout, rw, rb)


def _gather_rows(idx_smem, src_hbm, dst, sem, n):
    def issue(r, carry):
        pltpu.make_async_copy(src_hbm.at[idx_smem[0, r]], dst.at[r], sem).start()
        return carry
    lax.fori_loop(0, n, issue, 0)

    def drain(r, carry):
        pltpu.make_async_copy(src_hbm.at[0], dst.at[r], sem).wait()
        return carry
    lax.fori_loop(0, n, drain, 0)


def _expert_kernel(be_ref, nv_ref, tok_ref, h_hbm, guw_ref, gub_ref, dnw_ref, dnb_ref, y_ref,
                   idx_smem, xb, sem_i, sem_g):
    i = pl.program_id(0)

    @pl.when(i < nv_ref[0])
    def _():
        cp = pltpu.make_async_copy(tok_ref.at[0], idx_smem, sem_i)
        cp.start()
        cp.wait()
        _gather_rows(idx_smem, h_hbm, xb, sem_g, MOE_BM)
        x = xb[...].astype(bf16)
        gu = jnp.dot(x, guw_ref[0], preferred_element_type=f32) + gub_ref[0]
        glu = jnp.minimum(gu[:, :D_MODEL], SWIGLU_LIMIT)
        lin = jnp.clip(gu[:, D_MODEL:], -SWIGLU_LIMIT, SWIGLU_LIMIT)
        act = glu * _sigmoid(SWIGLU_ALPHA * glu) * (lin + 1.0)
        y_ref[...] = jnp.dot(act.astype(bf16), dnw_ref[0], preferred_element_type=f32) + dnb_ref[0]

    @pl.when(i >= nv_ref[0])
    def _():
        y_ref[...] = jnp.zeros_like(y_ref)


def _experts(block_expert, nvalid, slot_tok, h_rows, guw, gub, dnw, dnb):
    n_blocks = block_expert.shape[0]
    d = D_MODEL
    grid_spec = pltpu.PrefetchScalarGridSpec(
        num_scalar_prefetch=2,
        grid=(n_blocks,),
        in_specs=[pl.BlockSpec((1, 1, MOE_BM), lambda i, be, nv: (i, 0, 0)),
                  pl.BlockSpec(memory_space=pl.ANY),
                  pl.BlockSpec((1, d, 2 * d), lambda i, be, nv: (be[i], 0, 0)),
                  pl.BlockSpec((1, 1, 2 * d), lambda i, be, nv: (be[i], 0, 0)),
                  pl.BlockSpec((1, d, d), lambda i, be, nv: (be[i], 0, 0)),
                  pl.BlockSpec((1, 1, d), lambda i, be, nv: (be[i], 0, 0))],
        out_specs=pl.BlockSpec((MOE_BM, d), lambda i, be, nv: (i, 0)),
        scratch_shapes=[pltpu.SMEM((1, MOE_BM), jnp.int32), pltpu.VMEM((MOE_BM, d), f32),
                        pltpu.SemaphoreType.DMA, pltpu.SemaphoreType.DMA],
    )
    return pl.pallas_call(
        _expert_kernel,
        grid_spec=grid_spec,
        out_shape=jax.ShapeDtypeStruct((n_blocks * MOE_BM, d), f32),
        compiler_params=_cparams(("arbitrary",)),
        name="moe_experts",
    )(block_expert, nvalid, slot_tok.reshape(n_blocks, 1, MOE_BM), h_rows, guw,
      gub.reshape(N_EXPERTS, 1, 2 * d), dnw, dnb.reshape(N_EXPERTS, 1, d))


def _combine_kernel(dest_ref, x_ref, mod_ref, tw_ref, fg_ref, yb_hbm, o_ref, idx_smem, buf, sem_i, sem_g, *, final):
    cp = pltpu.make_async_copy(dest_ref.at[0], idx_smem, sem_i)
    cp.start()
    cp.wait()
    _gather_rows(idx_smem, yb_hbm, buf, sem_g, TOP_K * TM)
    tw = tw_ref[...]
    y = None
    for j in range(TOP_K):
        t = tw[:, j:j + 1] * buf[TM * j:TM * (j + 1), :]
        y = t if y is None else y + t
    xn = x_ref[...] + mod_ref[0][5:6] * y
    if final:
        xn = xn * lax.rsqrt(jnp.mean(xn * xn, axis=-1, keepdims=True) + NORM_EPS) * fg_ref[...]
    o_ref[...] = xn


def _combine(dest_t, x2, modt, tw, fg, yb, tpb, final):
    m, d = x2.shape
    return pl.pallas_call(
        functools.partial(_combine_kernel, final=final),
        grid=(m // TM,),
        in_specs=[pl.BlockSpec((1, 1, TOP_K * TM), lambda i: (i, 0, 0)),
                  pl.BlockSpec((TM, d), lambda i: (i, 0)),
                  pl.BlockSpec((1, 8, d), lambda i: (_seg_of_tile(i, tpb), 0, 0)),
                  pl.BlockSpec((TM, LANES), lambda i: (i, 0)),
                  pl.BlockSpec((1, d), lambda i: (0, 0)),
                  pl.BlockSpec(memory_space=pl.ANY)],
        out_specs=pl.BlockSpec((TM, d), lambda i: (i, 0)),
        out_shape=jax.ShapeDtypeStruct((m, d), f32),
        scratch_shapes=[pltpu.SMEM((1, TOP_K * TM), jnp.int32), pltpu.VMEM((TOP_K * TM, d), f32),
                        pltpu.SemaphoreType.DMA, pltpu.SemaphoreType.DMA],
        compiler_params=_cparams(("arbitrary",)),
        name="moe_combine",
    )(dest_t, x2, modt, tw, fg.reshape(1, d), yb)


def _dispatch(idx4, m):
    n_assign = m * TOP_K
    e_flat = idx4.reshape(-1)
    onehot = (e_flat[:, None] == jnp.arange(N_EXPERTS, dtype=jnp.int32)[None, :]).astype(jnp.int32)
    csum = jnp.cumsum(onehot, axis=0)
    rank = jnp.take_along_axis(csum, e_flat[:, None], axis=1)[:, 0] - 1
    counts = csum[-1]
    padded = (counts + MOE_BM - 1) // MOE_BM * MOE_BM
    pad_end = jnp.cumsum(padded)
    pad_start = pad_end - padded
    dest = (pad_start[e_flat] + rank).astype(jnp.int32)
    n_blocks = -(-n_assign // MOE_BM) + N_EXPERTS
    slot_tok = jnp.zeros((n_blocks * MOE_BM,), jnp.int32).at[dest].set(
        jnp.arange(n_assign, dtype=jnp.int32) // TOP_K)
    block_expert = jnp.minimum(
        jnp.searchsorted(pad_end, jnp.arange(n_blocks, dtype=jnp.int32) * MOE_BM, side='right'),
        N_EXPERTS - 1).astype(jnp.int32)
    nvalid = (pad_end[-1:] // MOE_BM).astype(jnp.int32)
    return dest, slot_tok, block_expert, nvalid


def _layer(x2, modt, lp, b, tpb, ridx, final_g, last):
    m = x2.shape[0]
    ttot = m // b
    p = _proj_in(x2, lp['norm1_g'], modt, lp['w7'], tpb)

    (r, v, kk, d0, d1, b0, b1, k0, k1, gg, bonus) = _rwkv_prep(
        p, lp['mu'], lp['vec'], lp['w2p'], lp['a2p'], lp['g2'], lp['e'], tpb)
    kk_rows = _to_scan_rows((kk, kk), ridx, b)
    kkn = jnp.concatenate([kk_rows[1:], jnp.zeros_like(kk_rows[:1])], axis=0)
    y_scan = _rwkv_scan(_to_scan_rows((d0, d1), ridx, b), _to_scan_rows((b0, b1), ridx, b),
                        _to_scan_rows((k0, k1), ridx, b), kkn, _to_scan_rows((r, r), ridx, b),
                        _to_scan_v(v, ridx, b))
    yr = _from_scan_y(y_scan, ridx, b)

    na = _na(p, lp['na_bias'], b, tpb)

    u = p[:, CB_U * BRANCH_W:(CB_U + 1) * BRANCH_W]
    u_tb = jnp.transpose(u.reshape(b, ttot, BRANCH_W), (1, 0, 2)).reshape(ttot * b, BRANCH_W)
    to_bt = lambda y: jnp.transpose(y.reshape(ttot, b, BRANCH_W), (1, 0, 2)).reshape(m, BRANCH_W)
    y5f = to_bt(_s5_scan(u_tb, *lp['s5'][0], b, tpb, False))
    y5b = to_bt(_s5_scan(u_tb, *lp['s5'][1], b, tpb, True))

    xn, h2, idx, tw = _merge(x2, modt, lp['norm2_g'], yr, bonus, gg, na, y5f, y5b, p, lp['lnv'], lp['e'],
                             lp['gluw'], lp['wbr'], lp['wout'], lp['rw'], lp['rb'], tpb)

    dest, slot_tok, block_expert, nvalid = _dispatch(idx[:, :TOP_K], m)
    yb = _experts(block_expert, nvalid, slot_tok, h2, lp['guw'], lp['gub'], lp['dnw'], lp['dnb'])
    dest_t = jnp.transpose(dest.reshape(m // TM, TM, TOP_K), (0, 2, 1)).reshape(m // TM, 1, TOP_K * TM)
    return _combine(dest_t, xn, modt, tw, final_g, yb, tpb, last)


def _pad_rows(w, before, total):
    return jnp.pad(w, ((0, 0), (before, total - before - w.shape[1]), (0, 0)))


def kernel(x, c, ctx, c_ctx, ada_w, ada_b, norm1_g, norm2_g, w_in, rwkv_mu_prev, rwkv_mu_next, rwkv_w0, rwkv_w2, rwkv_a0, rwkv_a2, rwkv_g2, rwkv_k_k, rwkv_k_a, rwkv_r_k, rwkv_ln_w, rwkv_ln_b, na_rpb, s5_lambda_re, s5_lambda_im, s5_log_dt, s5_b_re, s5_b_im, s5_c_re, s5_c_im, s5_d, s5_glu_w, s5_glu_b, w_branch, w_out, router_w, router_b, expert_gu_w, expert_gu_b, expert_dn_w, expert_dn_b, final_g):
    b, s, d = x.shape
    l = ctx.shape[1]
    depth = ada_w.shape[0]
    assert d == D_MODEL and l == TM and s % TM == 0 and (s // GRID_W) >= 2 * (TM // GRID_W) and 8 % b == 0
    ttot = l + s
    tpb = ttot // TM
    m = b * ttot

    c8 = jnp.zeros((8, d), f32).at[:b].set(c).at[b].set(c_ctx)
    mods = _ada(c8, ada_w, ada_b).reshape(depth, 8, 6, d)
    ridx = jnp.concatenate([jnp.arange(l - 1, -1, -1), l + jnp.arange(s - 1, -1, -1)]).astype(jnp.int32)
    eye_h = jnp.kron(jnp.eye(HEADS, dtype=f32), jnp.ones((HEAD, HEAD), f32)).astype(bf16)
    zeros_bw = jnp.zeros((BRANCH_W,), f32)

    x2 = jnp.concatenate([ctx, x], axis=1).reshape(m, d)
    for i in range(depth):
        mi = mods[i]
        modt = jnp.stack([jnp.broadcast_to(mi[b], (b, 6, d)), mi[:b]], axis=1).reshape(2 * b, 6, d)
        modt = jnp.pad(modt, ((0, 0), (0, 2), (0, 0)))
        w = w_in[i]
        w7 = jnp.concatenate([w[:, :RWKV_COLS], jnp.zeros((d, COL_PAD), f32), w[:, RWKV_COLS:]], axis=1).astype(bf16)
        rw = jnp.pad(router_w[i], ((0, 0), (0, LANES - N_EXPERTS)))
        rw_hi = rw.astype(bf16)
        rw_lo = (rw - rw_hi.astype(f32)).astype(bf16)
        lp = {
            'norm1_g': norm1_g[i], 'norm2_g': norm2_g[i], 'w7': w7,
            'mu': jnp.stack([rwkv_mu_prev[i], rwkv_mu_next[i]], 0),
            'vec': jnp.stack([rwkv_k_k[i], rwkv_k_a[i], rwkv_r_k[i].reshape(-1), rwkv_w0[i, 0], rwkv_w0[i, 1],
                              rwkv_a0[i, 0], rwkv_a0[i, 1], zeros_bw], 0),
            'w2p': _pad_rows(rwkv_w2[i], 0, LANES).astype(bf16),
            'a2p': _pad_rows(rwkv_a2[i], LORA_W, LANES).astype(bf16),
            'g2': rwkv_g2[i].astype(bf16), 'e': eye_h,
            'na_bias': _na_bias_tables(na_rpb[i]),
            's5': [_s5_params(s5_lambda_re[i, dd], s5_lambda_im[i, dd], s5_log_dt[i, dd], s5_b_re[i, dd],
                              s5_b_im[i, dd], s5_c_re[i, dd], s5_c_im[i, dd]) for dd in range(2)],
            'lnv': jnp.stack([rwkv_ln_w[i], rwkv_ln_b[i], s5_d[i], s5_glu_b[i]] + [zeros_bw] * 4, 0),
            'gluw': s5_glu_w[i].astype(bf16), 'wbr': w_branch[i].astype(bf16), 'wout': w_out[i].astype(bf16),
            'rw': jnp.stack([rw_hi, rw_lo], 0), 'rb': jnp.pad(router_b[i], (0, LANES - N_EXPERTS)).reshape(1, LANES),
            'guw': expert_gu_w[i].astype(bf16), 'gub': expert_gu_b[i],
            'dnw': expert_dn_w[i].astype(bf16), 'dnb': expert_dn_b[i],
        }
        x2 = _layer(x2, modt, lp, b, tpb, ridx, final_g, i == depth - 1)
    return x2.reshape(b, ttot, d)[:, l:]
```

```python
import functools
import math

import numpy as np
import jax
import jax.numpy as jnp
from jax import lax
from jax.experimental import pallas as pl
from jax.experimental.pallas import tpu as pltpu

f32 = jnp.float32
bf16 = jnp.bfloat16

D_MODEL = 1024
BRANCH_W = 512
GRID_W = 64
HEADS = 8
HEAD = 64
LORA_W = 64
LORA_A = 64
LORA_G = 128
RWKV_COLS = 3 * BRANCH_W + LORA_W + LORA_A + LORA_G
RWKV_GN_EPS = 64e-5
NA_KH = 8
NA_KW = 16
S5_GROUP = 16
S5_GROUPS = BRANCH_W // S5_GROUP
S5_STATE = 64
N_EXPERTS = 32
TOP_K = 4
SWIGLU_LIMIT = 7.0
SWIGLU_ALPHA = 1.702
NORM_EPS = 1e-6

TM = 256
LANES = 128
COL_PAD = 256
P_COLS = RWKV_COLS + COL_PAD + 3 * BRANCH_W + BRANCH_W + 3 * D_MODEL
CB_Q, CB_K, CB_V, CB_U = 4, 5, 6, 7
CB_GATE = 4
IN_TN = 1792
NA_NEG = -1e30
S5_CB = 128
S5_NQ = BRANCH_W // S5_CB
S5_BS = (S5_CB // S5_GROUP) * S5_STATE
MOE_BM = 256
VMEM_LIMIT = 56 * 1024 * 1024


def _cparams(sem, **kw):
    return pltpu.CompilerParams(dimension_semantics=sem, vmem_limit_bytes=VMEM_LIMIT, **kw)


def _segsum(x, e):
    hi = x.astype(bf16)
    lo = (x - hi.astype(f32)).astype(bf16)
    return jnp.dot(hi, e, preferred_element_type=f32) + jnp.dot(lo, e, preferred_element_type=f32)


def _sigmoid(x):
    return 1.0 / (1.0 + jnp.exp(-x))


def _softplus(x):
    return jnp.maximum(x, 0.0) + jnp.log(1.0 + jnp.exp(-jnp.abs(x)))


def _rms_mod(x, g, scale, shift):
    y = x * lax.rsqrt(jnp.mean(x * x, axis=-1, keepdims=True) + NORM_EPS)
    return y * g * (1.0 + scale) + shift


def _ada_kernel(c_ref, w_ref, b_ref, o_ref):
    c = c_ref[...]
    act = (c * _sigmoid(c)).astype(bf16)
    o_ref[0] = jnp.dot(act, w_ref[0].astype(bf16), preferred_element_type=f32) + b_ref[0]


def _ada(c8, ada_w, ada_b):
    depth, d, n = ada_w.shape
    tn = 1536
    return pl.pallas_call(
        _ada_kernel,
        grid=(depth, n // tn),
        in_specs=[pl.BlockSpec((8, d), lambda l, j: (0, 0)),
                  pl.BlockSpec((1, d, tn), lambda l, j: (l, 0, j)),
                  pl.BlockSpec((1, 1, tn), lambda l, j: (l, 0, j))],
        out_specs=pl.BlockSpec((1, 8, tn), lambda l, j: (l, 0, j)),
        out_shape=jax.ShapeDtypeStruct((depth, 8, n), f32),
        compiler_params=_cparams(("parallel", "parallel")),
        name="ada_mod",
    )(c8, ada_w, ada_b.reshape(depth, 1, n))


def _in_kernel(x_ref, g_ref, mod_ref, w_ref, o_ref, h_s):
    @pl.when(pl.program_id(1) == 0)
    def _():
        mod = mod_ref[0]
        h_s[...] = _rms_mod(x_ref[...], g_ref[...], mod[1:2], mod[0:1]).astype(bf16)

    o_ref[...] = jnp.dot(h_s[...], w_ref[...], preferred_element_type=f32)


def _seg_of_tile(i, tpb):
    return 2 * (i // tpb) + jnp.minimum(i % tpb, 1)


def _proj_in(x2, g, modt, w7, tpb):
    m, d = x2.shape
    return pl.pallas_call(
        _in_kernel,
        grid=(m // TM, P_COLS // IN_TN),
        in_specs=[pl.BlockSpec((TM, d), lambda i, j: (i, 0)),
                  pl.BlockSpec((1, d), lambda i, j: (0, 0)),
                  pl.BlockSpec((1, 8, d), lambda i, j: (_seg_of_tile(i, tpb), 0, 0)),
                  pl.BlockSpec((d, IN_TN), lambda i, j: (0, j))],
        out_specs=pl.BlockSpec((TM, IN_TN), lambda i, j: (i, j)),
        out_shape=jax.ShapeDtypeStruct((m, P_COLS), f32),
        scratch_shapes=[pltpu.VMEM((TM, d), bf16)],
        compiler_params=_cparams(("parallel", "arbitrary")),
        name="proj_in",
    )(x2, g.reshape(1, d), modt, w7)


def _rwkv_prep_kernel(p_ref, pp_ref, pn_ref, mu_ref, vec_ref, w2_ref, a2_ref, g2_ref, e_ref,
                      r_o, v_o, kk_o, d0_o, d1_o, b0_o, b1_o, k0_o, k1_o, g_o, bonus_o, *, tpb):
    j = pl.program_id(0) % tpb
    prev_ok = jnp.logical_and(j != 0, j != 1)
    next_ok = jnp.logical_and(j != 0, j != tpb - 1)
    p = p_ref[...]
    row = lax.broadcasted_iota(jnp.int32, p.shape, 0)
    hp = jnp.where(prev_ok, pp_ref[7:8, :], 0.0)
    hn = jnp.where(next_ok, pn_ref[0:1, :], 0.0)
    prev = jnp.where(row == 0, hp, pltpu.roll(p, 1, 0))
    nxt = jnp.where(row == TM - 1, hn, pltpu.roll(p, TM - 1, 0))
    mu = mu_ref[...]
    z = p + mu[0:1] * (prev - p) + mu[1:2] * (nxt - p)

    r = z[:, 0:BRANCH_W]
    k = z[:, BRANCH_W:2 * BRANCH_W]
    v = z[:, 2 * BRANCH_W:3 * BRANCH_W]
    wa = z[:, 3 * BRANCH_W:3 * BRANCH_W + LORA_W + LORA_A]
    gd = z[:, 3 * BRANCH_W + LORA_W + LORA_A:RWKV_COLS]
    vec = vec_ref[...]
    e = e_ref[...]
    kk = k * vec[0:1]
    kk = kk / jnp.maximum(jnp.sqrt(_segsum(kk * kk, e)), 1e-12)
    g_o[...] = jnp.dot(_sigmoid(gd).astype(bf16), g2_ref[...], preferred_element_type=f32)
    tw = jnp.tanh(wa).astype(bf16)
    wab = wa.astype(bf16)
    ksum = None
    for d, (dec_o, b_o, k_o) in enumerate(((d0_o, b0_o, k0_o), (d1_o, b1_o, k1_o))):
        wl = vec[3 + d:4 + d] + jnp.dot(tw, w2_ref[d], preferred_element_type=f32)
        w = -_softplus(-wl) - 0.5
        dec_o[...] = jnp.exp(-jnp.exp(w))
        a = _sigmoid(vec[5 + d:6 + d] + jnp.dot(wab, a2_ref[d], preferred_element_type=f32))
        kd = k * (1.0 + (a - 1.0) * vec[1:2])
        b_o[...] = kk * a
        k_o[...] = kd
        ksum = kd if ksum is None else ksum + kd
    r_o[...] = r
    v_o[...] = v
    kk_o[...] = kk
    bonus_o[...] = _segsum(r * ksum * vec[2:3], e) * v


def _rwkv_prep(p, mu, vec, w2p, a2p, g2, e, tpb):
    m = p.shape[0]
    bw = BRANCH_W
    full = lambda shape: pl.BlockSpec(shape, lambda i: (0,) * len(shape))
    out = pl.BlockSpec((TM, bw), lambda i: (i, 0))
    return pl.pallas_call(
        functools.partial(_rwkv_prep_kernel, tpb=tpb),
        grid=(m // TM,),
        in_specs=[pl.BlockSpec((TM, RWKV_COLS), lambda i: (i, 0)),
                  pl.BlockSpec((8, RWKV_COLS), lambda i: (jnp.maximum(i * (TM // 8) - 1, 0), 0)),
                  pl.BlockSpec((8, RWKV_COLS), lambda i: (jnp.minimum((i + 1) * (TM // 8), m // 8 - 1), 0)),
                  full((2, RWKV_COLS)), full((8, bw)), full((2, LANES, bw)), full((2, LANES, bw)),
                  full((LORA_G, bw)), full((bw, bw))],
        out_specs=[out] * 11,
        out_shape=[jax.ShapeDtypeStruct((m, bw), f32)] * 11,
        compiler_params=_cparams(("parallel",)),
        name="rwkv_prep",
    )(p, p, p, mu, vec, w2p, a2p, g2, e)


def _rwkv_scan_kernel(w_ref, b_ref, k_ref, kkn_ref, r_ref, v_ref, y_ref, s_ref, sa_ref, *, tc):
    @pl.when(pl.program_id(0) == 0)
    def _():
        s_ref[...] = jnp.zeros_like(s_ref)
        sa_ref[...] = jnp.zeros_like(sa_ref)

    shape = sa_ref.shape

    def step(t, sa):
        v = v_ref[t]
        y = [jnp.zeros(shape, f32), jnp.zeros(shape, f32)]
        nsa = [jnp.zeros(shape, f32), jnp.zeros(shape, f32)]
        for k in range(HEAD):
            row = lambda ref: jnp.broadcast_to(ref[t, k:k + 1, :], shape)
            s_new = row(w_ref) * s_ref[k] - row(b_ref) * sa + row(k_ref) * v
            s_ref[k] = s_new
            y[k % 2] = y[k % 2] + row(r_ref) * s_new
            nsa[k % 2] = nsa[k % 2] + row(kkn_ref) * s_new
        y_ref[t] = y[0] + y[1]
        return nsa[0] + nsa[1]

    sa_ref[...] = lax.fori_loop(0, tc, step, sa_ref[...])


def _rwkv_scan(w, b, k, kkn, r, v, tc=32):
    ttot, _, lanes = w.shape
    vlo = v.shape[1]
    rows = pl.BlockSpec((tc, HEAD, lanes), lambda i: (i, 0, 0))
    tile = pl.BlockSpec((tc, vlo, lanes), lambda i: (i, 0, 0))
    return pl.pallas_call(
        functools.partial(_rwkv_scan_kernel, tc=tc),
        grid=(ttot // tc,),
        in_specs=[rows] * 5 + [tile],
        out_specs=tile,
        out_shape=jax.ShapeDtypeStruct((ttot, vlo, lanes), f32),
        scratch_shapes=[pltpu.VMEM((HEAD, vlo, lanes), f32), pltpu.VMEM((vlo, lanes), f32)],
        compiler_params=_cparams(("arbitrary",)),
        name="rwkv_scan",
    )(w, b, k, kkn, r, v)


def _rev_stream(a):
    return jnp.concatenate([jnp.flip(a[:, :TM], 1), jnp.flip(a[:, TM:], 1)], axis=1)


def _to_scan_rows(a, b):
    af, ab = a
    t = af.shape[0] // b
    af = af.reshape(b, t, HEADS, HEAD)
    ab = _rev_stream(ab.reshape(b, t, HEADS, HEAD))
    x = jnp.stack([af, ab], 0)
    x = jnp.transpose(x, (2, 4, 0, 1, 3)).reshape(t, HEAD, 2 * b * HEADS)
    return jnp.concatenate([x, x], axis=-1)


def _to_scan_v(a, b):
    t = a.shape[0] // b
    af = a.reshape(b, t, HEADS, 2, HEAD // 2)
    x = jnp.stack([af, _rev_stream(af)], 0)
    return jnp.transpose(x, (2, 5, 4, 0, 1, 3)).reshape(t, HEAD // 2, 4 * b * HEADS)


def _from_scan_y(y, b):
    t = y.shape[0]
    x = y.reshape(t, HEAD // 2, 2, 2, b, HEADS)
    x = jnp.transpose(x, (3, 4, 0, 5, 2, 1)).reshape(2, b, t, BRANCH_W)
    return (x[0] + _rev_stream(x[1])).reshape(b * t, BRANCH_W)


def _softmax_pv(q, ks, vs, biases):
    ss = []
    for kt, bias in zip(ks, biases):
        s = lax.dot_general(q, kt, (((1,), (1,)), ((), ())), preferred_element_type=f32)
        ss.append(s if bias is None else s + bias)
    m = ss[0].max(axis=-1, keepdims=True)
    for s in ss[1:]:
        m = jnp.maximum(m, s.max(axis=-1, keepdims=True))
    den = None
    acc = None
    for s, vt in zip(ss, vs):
        pr = jnp.exp(s - m)
        d = pr.sum(axis=-1, keepdims=True)
        o = jnp.dot(pr.astype(bf16), vt, preferred_element_type=f32)
        den = d if den is None else den + d
        acc = o if acc is None else acc + o
    return acc / den


def _na_kernel(q_ref, kc_ref, vc_ref, k0_ref, k1_ref, k2_ref, v0_ref, v1_ref, v2_ref, bm_ref, o_ref):
    j = pl.program_id(1)
    scale = HEAD ** -0.5

    @pl.when(j == 0)
    def _():
        for h in range(HEADS):
            sl = slice(HEAD * h, HEAD * (h + 1))
            q = (q_ref[:, sl] * scale).astype(bf16)
            o_ref[:, sl] = _softmax_pv(q, [kc_ref[:, sl].astype(bf16)], [vc_ref[:, sl].astype(bf16)], [None])

    @pl.when(j > 0)
    def _():
        for h in range(HEADS):
            sl = slice(HEAD * h, HEAD * (h + 1))
            q = (q_ref[:, sl] * scale).astype(bf16)
            ks = [r[:, sl].astype(bf16) for r in (k0_ref, k1_ref, k2_ref, kc_ref)]
            vs = [r[:, sl].astype(bf16) for r in (v0_ref, v1_ref, v2_ref, vc_ref)]
            biases = [bm_ref[0, h, :, TM * s:TM * (s + 1)] for s in range(3)] + [None]
            o_ref[:, sl] = _softmax_pv(q, ks, vs, biases)


def _na(p, biasmask, b, tpb):
    m = p.shape[0]
    nlb = tpb - 1
    bw = BRANCH_W

    def kv(col, slot):
        def imap(bi, j):
            jj = jnp.maximum(j - 1, 0)
            return (bi * tpb + 1 + jnp.clip(jj - 1 + slot, 0, nlb - 1), col)
        return pl.BlockSpec((TM, bw), imap)

    def variant(bi, j):
        jj = jnp.maximum(j - 1, 0)
        return (jnp.where(jj == 0, 0, jnp.where(jj == nlb - 1, 2, 1)), 0, 0, 0)

    return pl.pallas_call(
        _na_kernel,
        grid=(b, tpb),
        in_specs=[pl.BlockSpec((TM, bw), lambda bi, j: (bi * tpb + j, CB_Q)),
                  pl.BlockSpec((TM, bw), lambda bi, j: (bi * tpb, CB_K)),
                  pl.BlockSpec((TM, bw), lambda bi, j: (bi * tpb, CB_V)),
                  kv(CB_K, 0), kv(CB_K, 1), kv(CB_K, 2), kv(CB_V, 0), kv(CB_V, 1), kv(CB_V, 2),
                  pl.BlockSpec((1, HEADS, TM, 3 * TM), variant)],
        out_specs=pl.BlockSpec((TM, bw), lambda bi, j: (bi * tpb + j, 0)),
        out_shape=jax.ShapeDtypeStruct((m, bw), f32),
        compiler_params=_cparams(("parallel", "arbitrary")),
        name="na_attn",
    )(p, p, p, p, p, p, p, p, p, biasmask)


def _na_bias_tables(rpb):
    qr = TM // GRID_W
    dr = np.arange(qr)
    krel = np.arange(3 * qr)
    qc = np.arange(GRID_W)
    kc = np.arange(GRID_W)
    roff = krel[None, :] - dr[:, None] + (NA_KH - 1 - qr)
    coff = kc[None, :] - qc[:, None] + NA_KW - 1
    cs = np.clip(qc - NA_KW // 2, 0, GRID_W - NA_KW)
    colvalid = (kc[None, :] >= cs[:, None]) & (kc[None, :] < cs[:, None] + NA_KW)
    rv_int = (krel[None, :] >= dr[:, None]) & (krel[None, :] < dr[:, None] + NA_KH)
    rv_first = np.broadcast_to((krel >= qr) & (krel < qr + NA_KH), rv_int.shape)
    rv_last = np.broadcast_to(krel < NA_KH, rv_int.shape)
    rowvalid = np.stack([rv_first, rv_int, rv_last], 0)
    valid = rowvalid[:, :, None, :, None] & colvalid[None, None, :, None, :]
    ri = np.clip(roff, 0, 2 * NA_KH - 2)
    ci = np.clip(coff, 0, 2 * NA_KW - 2)
    bias = rpb[:, ri[:, None, :, None], ci[None, :, None, :]]
    tab = jnp.where(valid[:, None], bias[None], NA_NEG)
    return tab.reshape(3, HEADS, TM, 3 * TM).astype(f32)


def _s5_kernel(u_ref, wb_ref, wc_ref, a_ref, y_ref, x_s, st_s, *, nb, reverse):
    rows = x_s.shape[0]
    ntile = rows // 8
    nsub = 8 // nb
    ncc = S5_BS // LANES
    shift = (8 - nb) if reverse else nb

    @pl.when(pl.program_id(0) == 0)
    def _():
        st_s[...] = jnp.zeros_like(st_s)

    grp = lax.broadcasted_iota(jnp.int32, (8, LANES), 0) // nb
    for q in range(S5_NQ):
        u = u_ref[:, S5_CB * q:S5_CB * (q + 1)].astype(bf16)
        x_s[...] = jnp.dot(u, wb_ref[q], preferred_element_type=f32)
        ar = [jnp.broadcast_to(a_ref[q, 0:1, LANES * c:LANES * (c + 1)], (8, LANES)) for c in range(ncc)]
        ai = [jnp.broadcast_to(a_ref[q, 1:2, LANES * c:LANES * (c + 1)], (8, LANES)) for c in range(ncc)]

        def tile_step(i, carry):
            rt = (ntile - 1 - i) if reverse else i
            r0 = pl.multiple_of(rt * 8, 8)
            new = []
            for c in range(ncc):
                cr, ci = carry[2 * c], carry[2 * c + 1]
                re_sl = slice(LANES * c, LANES * (c + 1))
                im_sl = slice(S5_BS + LANES * c, S5_BS + LANES * (c + 1))
                br = x_s[pl.ds(r0, 8), re_sl]
                bi = x_s[pl.ds(r0, 8), im_sl]
                out_r = out_i = None
                order = range(nsub - 1, -1, -1) if reverse else range(nsub)
                for s in order:
                    pr = pltpu.roll(cr, shift, 0)
                    pi = pltpu.roll(ci, shift, 0)
                    cr = ar[c] * pr - ai[c] * pi + br
                    ci = ar[c] * pi + ai[c] * pr + bi
                    out_r = cr if out_r is None else jnp.where(grp == s, cr, out_r)
                    out_i = ci if out_i is None else jnp.where(grp == s, ci, out_i)
                x_s[pl.ds(r0, 8), re_sl] = out_r
                x_s[pl.ds(r0, 8), im_sl] = out_i
                new += [out_r, out_i]
            return tuple(new)

        init = tuple(st_s[q, cc] for cc in range(2 * ncc))
        fin = lax.fori_loop(0, ntile, tile_step, init)
        for cc in range(2 * ncc):
            st_s[q, cc] = fin[cc]
        y_ref[:, S5_CB * q:S5_CB * (q + 1)] = jnp.dot(x_s[...].astype(bf16), wc_ref[q],
                                                      preferred_element_type=f32)


def _s5_scan(u_tb, wb, wc, a, nb, tpb, reverse):
    rows = TM * nb
    if reverse:
        imap = lambda c: (jnp.where(c == 0, 0, tpb - c), 0)
    else:
        imap = lambda c: (c, 0)
    full = lambda shape: pl.BlockSpec(shape, lambda c: (0,) * len(shape))
    return pl.pallas_call(
        functools.partial(_s5_kernel, nb=nb, reverse=reverse),
        grid=(tpb,),
        in_specs=[pl.BlockSpec((rows, BRANCH_W), imap),
                  full((S5_NQ, S5_CB, 2 * S5_BS)), full((S5_NQ, 2 * S5_BS, S5_CB)), full((S5_NQ, 2, S5_BS))],
        out_specs=pl.BlockSpec((rows, BRANCH_W), imap),
        out_shape=jax.ShapeDtypeStruct(u_tb.shape, f32),
        scratch_shapes=[pltpu.VMEM((rows, 2 * S5_BS), f32),
                        pltpu.VMEM((S5_NQ, 2 * S5_BS // LANES, 8, LANES), f32)],
        compiler_params=_cparams(("arbitrary",)),
        name="s5_bwd" if reverse else "s5_fwd",
    )(u_tb, wb, wc, a)


def _s5_params(lam_re, lam_im, log_dt, b_re, b_im, c_re, c_im):
    dt = jnp.exp(log_dt)[:, None]
    mag = jnp.exp(lam_re * dt)
    ar = mag * jnp.cos(lam_im * dt)
    ai = mag * jnp.sin(lam_im * dt)
    den = lam_re * lam_re + lam_im * lam_im
    cr = ((ar - 1.0) * lam_re + ai * lam_im) / den
    ci = (ai * lam_re - (ar - 1.0) * lam_im) / den
    bbr = cr[..., None] * b_re - ci[..., None] * b_im
    bbi = cr[..., None] * b_im + ci[..., None] * b_re
    gl = S5_CB // S5_GROUP
    eye = jnp.eye(gl, dtype=f32)

    def wb_half(bb):
        x = bb.reshape(S5_NQ, gl, S5_STATE, S5_GROUP)
        x = jnp.einsum('qgpc,gh->qgchp', x, eye)
        return x.reshape(S5_NQ, S5_CB, S5_BS)

    def wc_half(cc):
        x = cc.reshape(S5_NQ, gl, S5_GROUP, S5_STATE)
        x = jnp.einsum('qgcp,gh->qhpgc', x, eye)
        return x.reshape(S5_NQ, S5_BS, S5_CB)

    wb = jnp.concatenate([wb_half(bbr), wb_half(bbi)], axis=2).astype(bf16)
    wc = jnp.concatenate([wc_half(c_re), wc_half(-c_im)], axis=1).astype(bf16)
    a = jnp.stack([ar.reshape(S5_NQ, S5_BS), ai.reshape(S5_NQ, S5_BS)], axis=1)
    return wb, wc, a


def _gelu_tanh(x):
    return 0.5 * x * (1.0 + jnp.tanh(math.sqrt(2.0 / math.pi) * (x + 0.044715 * x * x * x)))


def _merge_kernel(x_ref, mod_ref, g2n_ref, yr_ref, bonus_ref, gg_ref, na_ref, y5f_ref, y5b_ref, u_ref,
                  ga_ref, gn_ref, gs_ref, lnv_ref, e_ref, gluw_ref, wbr_ref, wout_ref, rw_ref, rb_ref,
                  xo_ref, h2_ref, idx_ref, tw_ref):
    mod = mod_ref[0]
    lnv = lnv_ref[...]
    e = e_ref[...]
    y = yr_ref[...]
    mu = _segsum(y, e) * (1.0 / HEAD)
    dlt = y - mu
    var = _segsum(dlt * dlt, e) * (1.0 / HEAD)
    a_out = (dlt * lax.rsqrt(var + RWKV_GN_EPS) * lnv[0:1] + lnv[1:2] + bonus_ref[...]) * gg_ref[...]
    ys = _gelu_tanh(y5f_ref[...] + y5b_ref[...] + u_ref[...] * lnv[2:3])
    glu = jnp.dot(ys.astype(bf16), gluw_ref[...], preferred_element_type=f32) + lnv[3:4]
    s_out = ys * _sigmoid(glu)
    mix = None
    for o, gate_ref, jdx in ((a_out, ga_ref, 0), (na_ref[...], gn_ref, 1), (s_out, gs_ref, 2)):
        t = _sigmoid(gate_ref[...]) * jnp.dot(o.astype(bf16), wbr_ref[jdx], preferred_element_type=f32)
        mix = t if mix is None else mix + t
    ol = jnp.dot(mix.astype(bf16), wout_ref[...], preferred_element_type=f32)
    xn = x_ref[...] + mod[2:3] * ol
    xo_ref[...] = xn
    h2 = _rms_mod(xn, g2n_ref[...], mod[4:5], mod[3:4])
    h2_ref[...] = h2
    hi = h2.astype(bf16)
    lo = (h2 - hi.astype(f32)).astype(bf16)
    logits = (jnp.dot(hi, rw_ref[0], preferred_element_type=f32) + jnp.dot(lo, rw_ref[0], preferred_element_type=f32)
              + jnp.dot(hi, rw_ref[1], preferred_element_type=f32) + rb_ref[...])
    lane = lax.broadcasted_iota(jnp.int32, logits.shape, 1)
    logits = jnp.where(lane < N_EXPERTS, logits, -jnp.inf)
    idx_acc = jnp.zeros(logits.shape, jnp.int32)
    val_acc = jnp.full(logits.shape, -jnp.inf, f32)
    top = None
    for j in range(TOP_K):
        mx = logits.max(axis=-1, keepdims=True)
        sel = jnp.min(jnp.where(logits == mx, lane, LANES), axis=-1, keepdims=True)
        idx_acc = jnp.where(lane == j, sel, idx_acc)
        val_acc = jnp.where(lane == j, mx, val_acc)
        logits = jnp.where(lane == sel, -jnp.inf, logits)
        top = mx if top is None else top
    ex = jnp.exp(val_acc - top)
    idx_ref[...] = idx_acc
    tw_ref[...] = ex / ex.sum(axis=-1, keepdims=True)


def _merge(x2, modt, g2n, yr, bonus, gg, na, y5f, y5b, p, lnv, e, gluw, wbr, wout, rw, rb, tpb):
    m, d = x2.shape
    bw = BRANCH_W
    tok = lambda w: pl.BlockSpec((TM, w), lambda i: (i, 0))
    full = lambda shape: pl.BlockSpec(shape, lambda i: (0,) * len(shape))
    gate = lambda jdx: pl.BlockSpec((TM, d), lambda i: (i, CB_GATE + jdx))
    return pl.pallas_call(
        _merge_kernel,
        grid=(m // TM,),
        in_specs=[tok(d), pl.BlockSpec((1, 8, d), lambda i: (_seg_of_tile(i, tpb), 0, 0)), full((1, d)),
                  tok(bw), tok(bw), tok(bw), tok(bw), tok(bw), tok(bw),
                  pl.BlockSpec((TM, bw), lambda i: (i, CB_U)), gate(0), gate(1), gate(2),
                  full((8, bw)), full((bw, bw)), full((bw, bw)), full((3, bw, d)), full((d, d)),
                  full((2, d, LANES)), full((1, LANES))],
        out_specs=[tok(d), tok(d), tok(LANES), tok(LANES)],
        out_shape=[jax.ShapeDtypeStruct((m, d), f32), jax.ShapeDtypeStruct((m, d), f32),
                   jax.ShapeDtypeStruct((m, LANES), jnp.int32), jax.ShapeDtypeStruct((m, LANES), f32)],
        compiler_params=_cparams(("parallel",)),
        name="merge_router",
    )(x2, modt, g2n.reshape(1, d), yr, bonus, gg, na, y5f, y5b, p, p, p, p, lnv, e, gluw, wbr, wout, rw, rb)


def _start_row_gather(idx_vmem, idx_smem, src_hbm, dst, sem_i, sem_g, n):
    cp = pltpu.make_async_copy(idx_vmem, idx_smem, sem_i)
    cp.start()
    cp.wait()

    def issue(r, carry):
        pltpu.make_async_copy(src_hbm.at[idx_smem[0, r]], dst.at[r], sem_g).start()
        return carry
    lax.fori_loop(0, n, issue, 0, unroll=8)


def _wait_row_gather(src_hbm, dst, sem_g, n):
    pltpu.make_async_copy(src_hbm.at[pl.ds(0, n)], dst, sem_g).wait()


def _expert_kernel(be_ref, nv_ref, tok_ref, tokn_ref, h_hbm, guw_ref, gub_ref, dnw_ref, dnb_ref, y_ref,
                   idx_smem, xb, sem_i, sem_g):
    i = pl.program_id(0)
    nv = nv_ref[0]
    slot = i % 2

    @pl.when(jnp.logical_and(i == 0, nv > 0))
    def _():
        _start_row_gather(tok_ref.at[0], idx_smem, h_hbm, xb.at[0], sem_i, sem_g.at[0], MOE_BM)

    @pl.when(i + 1 < nv)
    def _():
        _start_row_gather(tokn_ref.at[0], idx_smem, h_hbm, xb.at[1 - slot], sem_i, sem_g.at[1 - slot], MOE_BM)

    @pl.when(i < nv)
    def _():
        _wait_row_gather(h_hbm, xb.at[slot], sem_g.at[slot], MOE_BM)
        x = xb[slot].astype(bf16)
        gu = jnp.dot(x, guw_ref[0], preferred_element_type=f32) + gub_ref[0]
        glu = jnp.minimum(gu[:, :D_MODEL], SWIGLU_LIMIT)
        lin = jnp.clip(gu[:, D_MODEL:], -SWIGLU_LIMIT, SWIGLU_LIMIT)
        act = glu * _sigmoid(SWIGLU_ALPHA * glu) * (lin + 1.0)
        y_ref[...] = jnp.dot(act.astype(bf16), dnw_ref[0], preferred_element_type=f32) + dnb_ref[0]

    @pl.when(i >= nv_ref[0])
    def _():
        y_ref[...] = jnp.zeros_like(y_ref)


def _experts(block_expert, nvalid, slot_tok, h_rows, guw, gub, dnw, dnb):
    n_blocks = block_expert.shape[0]
    d = D_MODEL
    grid_spec = pltpu.PrefetchScalarGridSpec(
        num_scalar_prefetch=2,
        grid=(n_blocks,),
        in_specs=[pl.BlockSpec((1, 1, MOE_BM), lambda i, be, nv: (i, 0, 0)),
                  pl.BlockSpec((1, 1, MOE_BM), lambda i, be, nv: (jnp.minimum(i + 1, n_blocks - 1), 0, 0)),
                  pl.BlockSpec(memory_space=pl.ANY),
                  pl.BlockSpec((1, d, 2 * d), lambda i, be, nv: (be[i], 0, 0)),
                  pl.BlockSpec((1, 1, 2 * d), lambda i, be, nv: (be[i], 0, 0)),
                  pl.BlockSpec((1, d, d), lambda i, be, nv: (be[i], 0, 0)),
                  pl.BlockSpec((1, 1, d), lambda i, be, nv: (be[i], 0, 0))],
        out_specs=pl.BlockSpec((MOE_BM, d), lambda i, be, nv: (i, 0)),
        scratch_shapes=[pltpu.SMEM((1, MOE_BM), jnp.int32), pltpu.VMEM((2, MOE_BM, d), f32),
                        pltpu.SemaphoreType.DMA, pltpu.SemaphoreType.DMA((2,))],
    )
    tok3 = slot_tok.reshape(n_blocks, 1, MOE_BM)
    return pl.pallas_call(
        _expert_kernel,
        grid_spec=grid_spec,
        out_shape=jax.ShapeDtypeStruct((n_blocks * MOE_BM, d), f32),
        compiler_params=_cparams(("arbitrary",), disable_bounds_checks=True),
        name="moe_experts",
    )(block_expert, nvalid, tok3, tok3, h_rows, guw,
      gub.reshape(N_EXPERTS, 1, 2 * d), dnw, dnb.reshape(N_EXPERTS, 1, d))


def _combine_kernel(dest_ref, x_ref, mod_ref, tw_ref, fg_ref, yb_hbm, o_ref, idx_smem, buf, sem_i, sem_g, *, final):
    _start_row_gather(dest_ref.at[0], idx_smem, yb_hbm, buf, sem_i, sem_g, TOP_K * TM)
    _wait_row_gather(yb_hbm, buf, sem_g, TOP_K * TM)
    tw = tw_ref[...]
    y = None
    for j in range(TOP_K):
        t = tw[:, j:j + 1] * buf[TM * j:TM * (j + 1), :]
        y = t if y is None else y + t
    xn = x_ref[...] + mod_ref[0][5:6] * y
    if final:
        xn = xn * lax.rsqrt(jnp.mean(xn * xn, axis=-1, keepdims=True) + NORM_EPS) * fg_ref[...]
    o_ref[...] = xn


def _combine(dest_t, x2, modt, tw, fg, yb, tpb, final):
    m, d = x2.shape
    return pl.pallas_call(
        functools.partial(_combine_kernel, final=final),
        grid=(m // TM,),
        in_specs=[pl.BlockSpec((1, 1, TOP_K * TM), lambda i: (i, 0, 0)),
                  pl.BlockSpec((TM, d), lambda i: (i, 0)),
                  pl.BlockSpec((1, 8, d), lambda i: (_seg_of_tile(i, tpb), 0, 0)),
                  pl.BlockSpec((TM, LANES), lambda i: (i, 0)),
                  pl.BlockSpec((1, d), lambda i: (0, 0)),
                  pl.BlockSpec(memory_space=pl.ANY)],
        out_specs=pl.BlockSpec((TM, d), lambda i: (i, 0)),
        out_shape=jax.ShapeDtypeStruct((m, d), f32),
        scratch_shapes=[pltpu.SMEM((1, TOP_K * TM), jnp.int32), pltpu.VMEM((TOP_K * TM, d), f32),
                        pltpu.SemaphoreType.DMA, pltpu.SemaphoreType.DMA],
        compiler_params=_cparams(("arbitrary",), disable_bounds_checks=True),
        name="moe_combine",
    )(dest_t, x2, modt, tw, fg.reshape(1, d), yb)


def _dispatch(idx4, m):
    n_assign = m * TOP_K
    e_flat = idx4.reshape(-1)
    onehot = (e_flat[:, None] == jnp.arange(N_EXPERTS, dtype=jnp.int32)[None, :]).astype(jnp.int32)
    csum = jnp.cumsum(onehot, axis=0)
    rank = jnp.take_along_axis(csum, e_flat[:, None], axis=1)[:, 0] - 1
    counts = csum[-1]
    padded = (counts + MOE_BM - 1) // MOE_BM * MOE_BM
    pad_end = jnp.cumsum(padded)
    pad_start = pad_end - padded
    dest = (pad_start[e_flat] + rank).astype(jnp.int32)
    n_blocks = -(-n_assign // MOE_BM) + N_EXPERTS
    slot_tok = jnp.zeros((n_blocks * MOE_BM,), jnp.int32).at[dest].set(
        jnp.arange(n_assign, dtype=jnp.int32) // TOP_K)
    block_expert = jnp.minimum(
        jnp.searchsorted(pad_end, jnp.arange(n_blocks, dtype=jnp.int32) * MOE_BM, side='right'),
        N_EXPERTS - 1).astype(jnp.int32)
    nvalid = (pad_end[-1:] // MOE_BM).astype(jnp.int32)
    return dest, slot_tok, block_expert, nvalid


def _layer(x2, modt, lp, b, tpb, final_g, last):
    m = x2.shape[0]
    ttot = m // b
    p = _proj_in(x2, lp['norm1_g'], modt, lp['w7'], tpb)

    (r, v, kk, d0, d1, b0, b1, k0, k1, gg, bonus) = _rwkv_prep(
        p, lp['mu'], lp['vec'], lp['w2p'], lp['a2p'], lp['g2'], lp['e'], tpb)
    kk_rows = _to_scan_rows((kk, kk), b)
    kkn = jnp.concatenate([kk_rows[1:], jnp.zeros_like(kk_rows[:1])], axis=0)
    y_scan = _rwkv_scan(_to_scan_rows((d0, d1), b), _to_scan_rows((b0, b1), b),
                        _to_scan_rows((k0, k1), b), kkn, _to_scan_rows((r, r), b), _to_scan_v(v, b))
    yr = _from_scan_y(y_scan, b)

    na = _na(p, lp['na_bias'], b, tpb)

    u = p[:, CB_U * BRANCH_W:(CB_U + 1) * BRANCH_W]
    u_tb = jnp.transpose(u.reshape(b, ttot, BRANCH_W), (1, 0, 2)).reshape(ttot * b, BRANCH_W)
    to_bt = lambda y: jnp.transpose(y.reshape(ttot, b, BRANCH_W), (1, 0, 2)).reshape(m, BRANCH_W)
    y5f = to_bt(_s5_scan(u_tb, *lp['s5'][0], b, tpb, False))
    y5b = to_bt(_s5_scan(u_tb, *lp['s5'][1], b, tpb, True))

    xn, h2, idx, tw = _merge(x2, modt, lp['norm2_g'], yr, bonus, gg, na, y5f, y5b, p, lp['lnv'], lp['e'],
                             lp['gluw'], lp['wbr'], lp['wout'], lp['rw'], lp['rb'], tpb)

    dest, slot_tok, block_expert, nvalid = _dispatch(idx[:, :TOP_K], m)
    yb = _experts(block_expert, nvalid, slot_tok, h2, lp['guw'], lp['gub'], lp['dnw'], lp['dnb'])
    dest_t = jnp.transpose(dest.reshape(m // TM, TM, TOP_K), (0, 2, 1)).reshape(m // TM, 1, TOP_K * TM)
    return _combine(dest_t, xn, modt, tw, final_g, yb, tpb, last)


def _pad_rows(w, before, total):
    return jnp.pad(w, ((0, 0), (before, total - before - w.shape[1]), (0, 0)))


def kernel(x, c, ctx, c_ctx, ada_w, ada_b, norm1_g, norm2_g, w_in, rwkv_mu_prev, rwkv_mu_next, rwkv_w0, rwkv_w2, rwkv_a0, rwkv_a2, rwkv_g2, rwkv_k_k, rwkv_k_a, rwkv_r_k, rwkv_ln_w, rwkv_ln_b, na_rpb, s5_lambda_re, s5_lambda_im, s5_log_dt, s5_b_re, s5_b_im, s5_c_re, s5_c_im, s5_d, s5_glu_w, s5_glu_b, w_branch, w_out, router_w, router_b, expert_gu_w, expert_gu_b, expert_dn_w, expert_dn_b, final_g):
    b, s, d = x.shape
    l = ctx.shape[1]
    depth = ada_w.shape[0]
    assert d == D_MODEL and l == TM and s % TM == 0 and (s // GRID_W) >= 2 * (TM // GRID_W) and 8 % b == 0
    ttot = l + s
    tpb = ttot // TM
    m = b * ttot

    c8 = jnp.zeros((8, d), f32).at[:b].set(c).at[b].set(c_ctx)
    mods = _ada(c8, ada_w, ada_b).reshape(depth, 8, 6, d)
    eye_h = jnp.kron(jnp.eye(HEADS, dtype=f32), jnp.ones((HEAD, HEAD), f32)).astype(bf16)
    zeros_bw = jnp.zeros((BRANCH_W,), f32)

    x2 = jnp.concatenate([ctx, x], axis=1).reshape(m, d)
    for i in range(depth):
        mi = mods[i]
        modt = jnp.stack([jnp.broadcast_to(mi[b], (b, 6, d)), mi[:b]], axis=1).reshape(2 * b, 6, d)
        modt = jnp.pad(modt, ((0, 0), (0, 2), (0, 0)))
        w = w_in[i]
        w7 = jnp.concatenate([w[:, :RWKV_COLS], jnp.zeros((d, COL_PAD), f32), w[:, RWKV_COLS:]], axis=1).astype(bf16)
        rw = jnp.pad(router_w[i], ((0, 0), (0, LANES - N_EXPERTS)))
        rw_hi = rw.astype(bf16)
        rw_lo = (rw - rw_hi.astype(f32)).astype(bf16)
        lp = {
            'norm1_g': norm1_g[i], 'norm2_g': norm2_g[i], 'w7': w7,
            'mu': jnp.stack([rwkv_mu_prev[i], rwkv_mu_next[i]], 0),
            'vec': jnp.stack([rwkv_k_k[i], rwkv_k_a[i], rwkv_r_k[i].reshape(-1), rwkv_w0[i, 0], rwkv_w0[i, 1],
                              rwkv_a0[i, 0], rwkv_a0[i, 1], zeros_bw], 0),
            'w2p': _pad_rows(rwkv_w2[i], 0, LANES).astype(bf16),
            'a2p': _pad_rows(rwkv_a2[i], LORA_W, LANES).astype(bf16),
            'g2': rwkv_g2[i].astype(bf16), 'e': eye_h,
            'na_bias': _na_bias_tables(na_rpb[i]),
            's5': [_s5_params(s5_lambda_re[i, dd], s5_lambda_im[i, dd], s5_log_dt[i, dd], s5_b_re[i, dd],
                              s5_b_im[i, dd], s5_c_re[i, dd], s5_c_im[i, dd]) for dd in range(2)],
            'lnv': jnp.stack([rwkv_ln_w[i], rwkv_ln_b[i], s5_d[i], s5_glu_b[i]] + [zeros_bw] * 4, 0),
            'gluw': s5_glu_w[i].astype(bf16), 'wbr': w_branch[i].astype(bf16), 'wout': w_out[i].astype(bf16),
            'rw': jnp.stack([rw_hi, rw_lo], 0), 'rb': jnp.pad(router_b[i], (0, LANES - N_EXPERTS)).reshape(1, LANES),
            'guw': expert_gu_w[i].astype(bf16), 'gub': expert_gu_b[i],
            'dnw': expert_dn_w[i].astype(bf16), 'dnb': expert_dn_b[i],
        }
        x2 = _layer(x2, modt, lp, b, tpb, final_g, i == depth - 1)
    return x2.reshape(b, ttot, d)[:, l:]
```

```python
import functools
import math

import numpy as np
import jax
import jax.numpy as jnp
from jax import lax
from jax.experimental import pallas as pl
from jax.experimental.pallas import tpu as pltpu

f32 = jnp.float32
bf16 = jnp.bfloat16

D_MODEL = 1024
BRANCH_W = 512
GRID_W = 64
HEADS = 8
HEAD = 64
LORA_W = 64
LORA_A = 64
LORA_G = 128
RWKV_COLS = 3 * BRANCH_W + LORA_W + LORA_A + LORA_G
RWKV_GN_EPS = 64e-5
NA_KH = 8
NA_KW = 16
S5_GROUP = 16
S5_GROUPS = BRANCH_W // S5_GROUP
S5_STATE = 64
N_EXPERTS = 32
TOP_K = 4
SWIGLU_LIMIT = 7.0
SWIGLU_ALPHA = 1.702
NORM_EPS = 1e-6

TM = 256
LANES = 128
COL_PAD = 256
P_COLS = RWKV_COLS + COL_PAD + 3 * BRANCH_W + BRANCH_W + 3 * D_MODEL
CB_Q, CB_K, CB_V, CB_U = 4, 5, 6, 7
CB_GATE = 4
IN_TN = 1792
NA_NEG = -1e30
S5_CB = 128
S5_NQ = BRANCH_W // S5_CB
S5_BS = (S5_CB // S5_GROUP) * S5_STATE
MOE_BM = 256
VMEM_LIMIT = 56 * 1024 * 1024


def _cparams(sem, **kw):
    return pltpu.CompilerParams(dimension_semantics=sem, vmem_limit_bytes=VMEM_LIMIT, **kw)


def _segsum(x, e):
    hi = x.astype(bf16)
    lo = (x - hi.astype(f32)).astype(bf16)
    return jnp.dot(hi, e, preferred_element_type=f32) + jnp.dot(lo, e, preferred_element_type=f32)


def _sigmoid(x):
    return 1.0 / (1.0 + jnp.exp(-x))


def _softplus(x):
    return jnp.maximum(x, 0.0) + jnp.log(1.0 + jnp.exp(-jnp.abs(x)))


def _rms_mod(x, g, scale, shift):
    y = x * lax.rsqrt(jnp.mean(x * x, axis=-1, keepdims=True) + NORM_EPS)
    return y * g * (1.0 + scale) + shift


def _ada_kernel(c_ref, w_ref, b_ref, o_ref):
    c = c_ref[...]
    act = (c * _sigmoid(c)).astype(bf16)
    o_ref[0] = jnp.dot(act, w_ref[0].astype(bf16), preferred_element_type=f32) + b_ref[0]


def _ada(c8, ada_w, ada_b):
    depth, d, n = ada_w.shape
    tn = 1536
    return pl.pallas_call(
        _ada_kernel,
        grid=(depth, n // tn),
        in_specs=[pl.BlockSpec((8, d), lambda l, j: (0, 0)),
                  pl.BlockSpec((1, d, tn), lambda l, j: (l, 0, j)),
                  pl.BlockSpec((1, 1, tn), lambda l, j: (l, 0, j))],
        out_specs=pl.BlockSpec((1, 8, tn), lambda l, j: (l, 0, j)),
        out_shape=jax.ShapeDtypeStruct((depth, 8, n), f32),
        compiler_params=_cparams(("parallel", "parallel")),
        name="ada_mod",
    )(c8, ada_w, ada_b.reshape(depth, 1, n))


def _in_kernel(x_ref, g_ref, mod_ref, w_ref, o_ref):
    mod = mod_ref[0]
    h = _rms_mod(x_ref[...], g_ref[...], mod[1:2], mod[0:1]).astype(bf16)
    o_ref[...] = jnp.dot(h, w_ref[...], preferred_element_type=f32)


def _seg_of_tile(i, tpb):
    return 2 * (i // tpb) + jnp.minimum(i % tpb, 1)


def _proj_in(x2, g, modt, w7, tpb):
    m, d = x2.shape
    return pl.pallas_call(
        _in_kernel,
        grid=(P_COLS // IN_TN, m // TM),
        in_specs=[pl.BlockSpec((TM, d), lambda j, i: (i, 0)),
                  pl.BlockSpec((1, d), lambda j, i: (0, 0)),
                  pl.BlockSpec((1, 8, d), lambda j, i: (_seg_of_tile(i, tpb), 0, 0)),
                  pl.BlockSpec((d, IN_TN), lambda j, i: (0, j))],
        out_specs=pl.BlockSpec((TM, IN_TN), lambda j, i: (i, j)),
        out_shape=jax.ShapeDtypeStruct((m, P_COLS), f32),
        compiler_params=_cparams(("parallel", "parallel")),
        name="proj_in",
    )(x2, g.reshape(1, d), modt, w7)


def _rwkv_prep_kernel(p_ref, pp_ref, pn_ref, mu_ref, vec_ref, w2_ref, a2_ref, g2_ref, e_ref,
                      r_o, v_o, kk_o, d0_o, d1_o, b0_o, b1_o, k0_o, k1_o, g_o, bonus_o, *, tpb):
    j = pl.program_id(0) % tpb
    prev_ok = jnp.logical_and(j != 0, j != 1)
    next_ok = jnp.logical_and(j != 0, j != tpb - 1)
    p = p_ref[...]
    row = lax.broadcasted_iota(jnp.int32, p.shape, 0)
    hp = jnp.where(prev_ok, pp_ref[7:8, :], 0.0)
    hn = jnp.where(next_ok, pn_ref[0:1, :], 0.0)
    prev = jnp.where(row == 0, hp, pltpu.roll(p, 1, 0))
    nxt = jnp.where(row == TM - 1, hn, pltpu.roll(p, TM - 1, 0))
    mu = mu_ref[...]
    z = p + mu[0:1] * (prev - p) + mu[1:2] * (nxt - p)

    r = z[:, 0:BRANCH_W]
    k = z[:, BRANCH_W:2 * BRANCH_W]
    v = z[:, 2 * BRANCH_W:3 * BRANCH_W]
    wa = z[:, 3 * BRANCH_W:3 * BRANCH_W + LORA_W + LORA_A]
    gd = z[:, 3 * BRANCH_W + LORA_W + LORA_A:RWKV_COLS]
    vec = vec_ref[...]
    e = e_ref[...]
    kk = k * vec[0:1]
    kk = kk / jnp.maximum(jnp.sqrt(_segsum(kk * kk, e)), 1e-12)
    g_o[...] = jnp.dot(_sigmoid(gd).astype(bf16), g2_ref[...], preferred_element_type=f32)
    tw = jnp.tanh(wa).astype(bf16)
    wab = wa.astype(bf16)
    ksum = None
    for d, (dec_o, b_o, k_o) in enumerate(((d0_o, b0_o, k0_o), (d1_o, b1_o, k1_o))):
        wl = vec[3 + d:4 + d] + jnp.dot(tw, w2_ref[d], preferred_element_type=f32)
        w = -_softplus(-wl) - 0.5
        dec_o[...] = jnp.exp(-jnp.exp(w))
        a = _sigmoid(vec[5 + d:6 + d] + jnp.dot(wab, a2_ref[d], preferred_element_type=f32))
        kd = k * (1.0 + (a - 1.0) * vec[1:2])
        b_o[...] = kk * a
        k_o[...] = kd
        ksum = kd if ksum is None else ksum + kd
    r_o[...] = r
    v_o[...] = v
    kk_o[...] = kk
    bonus_o[...] = _segsum(r * ksum * vec[2:3], e) * v


def _rwkv_prep(p, mu, vec, w2p, a2p, g2, e, tpb):
    m = p.shape[0]
    bw = BRANCH_W
    full = lambda shape: pl.BlockSpec(shape, lambda i: (0,) * len(shape))
    out = pl.BlockSpec((TM, bw), lambda i: (i, 0))
    return pl.pallas_call(
        functools.partial(_rwkv_prep_kernel, tpb=tpb),
        grid=(m // TM,),
        in_specs=[pl.BlockSpec((TM, RWKV_COLS), lambda i: (i, 0)),
                  pl.BlockSpec((8, RWKV_COLS), lambda i: (jnp.maximum(i * (TM // 8) - 1, 0), 0)),
                  pl.BlockSpec((8, RWKV_COLS), lambda i: (jnp.minimum((i + 1) * (TM // 8), m // 8 - 1), 0)),
                  full((2, RWKV_COLS)), full((8, bw)), full((2, LANES, bw)), full((2, LANES, bw)),
                  full((LORA_G, bw)), full((bw, bw))],
        out_specs=[out] * 11,
        out_shape=[jax.ShapeDtypeStruct((m, bw), f32)] * 11,
        compiler_params=_cparams(("parallel",)),
        name="rwkv_prep",
    )(p, p, p, mu, vec, w2p, a2p, g2, e)


def _rwkv_scan_kernel(w_ref, b_ref, k_ref, kkn_ref, r_ref, v_ref, y_ref, s_ref, sa_ref, *, tc):
    @pl.when(pl.program_id(0) == 0)
    def _():
        s_ref[...] = jnp.zeros_like(s_ref)
        sa_ref[...] = jnp.zeros_like(sa_ref)

    shape = sa_ref.shape

    def step(t, sa):
        v = v_ref[t]
        y = [jnp.zeros(shape, f32), jnp.zeros(shape, f32)]
        nsa = [jnp.zeros(shape, f32), jnp.zeros(shape, f32)]
        for k in range(HEAD):
            row = lambda ref: jnp.broadcast_to(ref[t, k:k + 1, :], shape)
            s_new = row(w_ref) * s_ref[k] - row(b_ref) * sa + row(k_ref) * v
            s_ref[k] = s_new
            y[k % 2] = y[k % 2] + row(r_ref) * s_new
            nsa[k % 2] = nsa[k % 2] + row(kkn_ref) * s_new
        y_ref[t] = y[0] + y[1]
        return nsa[0] + nsa[1]

    sa_ref[...] = lax.fori_loop(0, tc, step, sa_ref[...])


def _rwkv_scan(w, b, k, kkn, r, v, tc=32):
    ttot, _, lanes = w.shape
    vlo = v.shape[1]
    rows = pl.BlockSpec((tc, HEAD, lanes), lambda i: (i, 0, 0))
    tile = pl.BlockSpec((tc, vlo, lanes), lambda i: (i, 0, 0))
    return pl.pallas_call(
        functools.partial(_rwkv_scan_kernel, tc=tc),
        grid=(ttot // tc,),
        in_specs=[rows] * 5 + [tile],
        out_specs=tile,
        out_shape=jax.ShapeDtypeStruct((ttot, vlo, lanes), f32),
        scratch_shapes=[pltpu.VMEM((HEAD, vlo, lanes), f32), pltpu.VMEM((vlo, lanes), f32)],
        compiler_params=_cparams(("arbitrary",)),
        name="rwkv_scan",
    )(w, b, k, kkn, r, v)


def _rev_stream(a):
    return jnp.concatenate([jnp.flip(a[:, :TM], 1), jnp.flip(a[:, TM:], 1)], axis=1)


def _to_scan_rows(a, b):
    af, ab = a
    t = af.shape[0] // b
    af = af.reshape(b, t, HEADS, HEAD)
    ab = _rev_stream(ab.reshape(b, t, HEADS, HEAD))
    x = jnp.stack([af, ab], 0)
    x = jnp.transpose(x, (2, 4, 0, 1, 3)).reshape(t, HEAD, 2 * b * HEADS)
    return jnp.concatenate([x, x], axis=-1)


def _to_scan_v(a, b):
    t = a.shape[0] // b
    af = a.reshape(b, t, HEADS, 2, HEAD // 2)
    x = jnp.stack([af, _rev_stream(af)], 0)
    return jnp.transpose(x, (2, 5, 4, 0, 1, 3)).reshape(t, HEAD // 2, 4 * b * HEADS)


def _from_scan_y(y, b):
    t = y.shape[0]
    x = y.reshape(t, HEAD // 2, 2, 2, b, HEADS)
    x = jnp.transpose(x, (3, 4, 0, 5, 2, 1)).reshape(2, b, t, BRANCH_W)
    return (x[0] + _rev_stream(x[1])).reshape(b * t, BRANCH_W)


def _softmax_pv(q, ks, vs, biases):
    ss = []
    for kt, bias in zip(ks, biases):
        s = lax.dot_general(q, kt, (((1,), (1,)), ((), ())), preferred_element_type=f32)
        ss.append(s if bias is None else s + bias)
    m = ss[0].max(axis=-1, keepdims=True)
    for s in ss[1:]:
        m = jnp.maximum(m, s.max(axis=-1, keepdims=True))
    den = None
    acc = None
    for s, vt in zip(ss, vs):
        pr = jnp.exp(s - m)
        d = pr.sum(axis=-1, keepdims=True)
        o = jnp.dot(pr.astype(bf16), vt, preferred_element_type=f32)
        den = d if den is None else den + d
        acc = o if acc is None else acc + o
    return acc / den


def _na_kernel(q_ref, kc_ref, vc_ref, k0_ref, k1_ref, k2_ref, v0_ref, v1_ref, v2_ref, bm_ref, o_ref):
    j = pl.program_id(1)
    scale = HEAD ** -0.5

    @pl.when(j == 0)
    def _():
        for h in range(HEADS):
            sl = slice(HEAD * h, HEAD * (h + 1))
            q = (q_ref[:, sl] * scale).astype(bf16)
            o_ref[:, sl] = _softmax_pv(q, [kc_ref[:, sl].astype(bf16)], [vc_ref[:, sl].astype(bf16)], [None])

    @pl.when(j > 0)
    def _():
        for h in range(HEADS):
            sl = slice(HEAD * h, HEAD * (h + 1))
            q = (q_ref[:, sl] * scale).astype(bf16)
            ks = [r[:, sl].astype(bf16) for r in (k0_ref, k1_ref, k2_ref, kc_ref)]
            vs = [r[:, sl].astype(bf16) for r in (v0_ref, v1_ref, v2_ref, vc_ref)]
            biases = [bm_ref[0, h, :, TM * s:TM * (s + 1)] for s in range(3)] + [None]
            o_ref[:, sl] = _softmax_pv(q, ks, vs, biases)


def _na(p, biasmask, b, tpb):
    m = p.shape[0]
    nlb = tpb - 1
    bw = BRANCH_W

    def kv(col, slot):
        def imap(bi, j):
            jj = jnp.maximum(j - 1, 0)
            return (bi * tpb + 1 + jnp.clip(jj - 1 + slot, 0, nlb - 1), col)
        return pl.BlockSpec((TM, bw), imap)

    def variant(bi, j):
        jj = jnp.maximum(j - 1, 0)
        return (jnp.where(jj == 0, 0, jnp.where(jj == nlb - 1, 2, 1)), 0, 0, 0)

    return pl.pallas_call(
        _na_kernel,
        grid=(b, tpb),
        in_specs=[pl.BlockSpec((TM, bw), lambda bi, j: (bi * tpb + j, CB_Q)),
                  pl.BlockSpec((TM, bw), lambda bi, j: (bi * tpb, CB_K)),
                  pl.BlockSpec((TM, bw), lambda bi, j: (bi * tpb, CB_V)),
                  kv(CB_K, 0), kv(CB_K, 1), kv(CB_K, 2), kv(CB_V, 0), kv(CB_V, 1), kv(CB_V, 2),
                  pl.BlockSpec((1, HEADS, TM, 3 * TM), variant)],
        out_specs=pl.BlockSpec((TM, bw), lambda bi, j: (bi * tpb + j, 0)),
        out_shape=jax.ShapeDtypeStruct((m, bw), f32),
        compiler_params=_cparams(("parallel", "arbitrary")),
        name="na_attn",
    )(p, p, p, p, p, p, p, p, p, biasmask)


def _na_bias_tables(rpb):
    qr = TM // GRID_W
    dr = np.arange(qr)
    krel = np.arange(3 * qr)
    qc = np.arange(GRID_W)
    kc = np.arange(GRID_W)
    roff = krel[None, :] - dr[:, None] + (NA_KH - 1 - qr)
    coff = kc[None, :] - qc[:, None] + NA_KW - 1
    cs = np.clip(qc - NA_KW // 2, 0, GRID_W - NA_KW)
    colvalid = (kc[None, :] >= cs[:, None]) & (kc[None, :] < cs[:, None] + NA_KW)
    rv_int = (krel[None, :] >= dr[:, None]) & (krel[None, :] < dr[:, None] + NA_KH)
    rv_first = np.broadcast_to((krel >= qr) & (krel < qr + NA_KH), rv_int.shape)
    rv_last = np.broadcast_to(krel < NA_KH, rv_int.shape)
    rowvalid = np.stack([rv_first, rv_int, rv_last], 0)
    valid = rowvalid[:, :, None, :, None] & colvalid[None, None, :, None, :]
    r1 = (roff[:, :, None] == np.arange(2 * NA_KH - 1)).astype(np.float32)
    c1 = (coff[:, :, None] == np.arange(2 * NA_KW - 1)).astype(np.float32)
    tmp = jnp.einsum('rka,hab->hrkb', r1, rpb, precision=lax.Precision.HIGHEST)
    bias = jnp.einsum('hrkb,qcb->hrqkc', tmp, c1, precision=lax.Precision.HIGHEST)
    tab = jnp.where(valid[:, None], bias[None], NA_NEG)
    return tab.reshape(3, HEADS, TM, 3 * TM).astype(f32)


def _s5_kernel(u_ref, wb_ref, wc_ref, a_ref, y_ref, x_s, st_s, *, nb, reverse):
    rows = x_s.shape[0]
    ntile = rows // 8
    nsub = 8 // nb
    ncc = S5_BS // LANES
    shift = (8 - nb) if reverse else nb

    @pl.when(pl.program_id(0) == 0)
    def _():
        st_s[...] = jnp.zeros_like(st_s)

    grp = lax.broadcasted_iota(jnp.int32, (8, LANES), 0) // nb
    for q in range(S5_NQ):
        u = u_ref[:, S5_CB * q:S5_CB * (q + 1)].astype(bf16)
        x_s[...] = jnp.dot(u, wb_ref[q], preferred_element_type=f32)
        ar = [jnp.broadcast_to(a_ref[q, 0:1, LANES * c:LANES * (c + 1)], (8, LANES)) for c in range(ncc)]
        ai = [jnp.broadcast_to(a_ref[q, 1:2, LANES * c:LANES * (c + 1)], (8, LANES)) for c in range(ncc)]

        def tile_step(i, carry):
            rt = (ntile - 1 - i) if reverse else i
            r0 = pl.multiple_of(rt * 8, 8)
            new = []
            for c in range(ncc):
                cr, ci = carry[2 * c], carry[2 * c + 1]
                re_sl = slice(LANES * c, LANES * (c + 1))
                im_sl = slice(S5_BS + LANES * c, S5_BS + LANES * (c + 1))
                br = x_s[pl.ds(r0, 8), re_sl]
                bi = x_s[pl.ds(r0, 8), im_sl]
                out_r = out_i = None
                order = range(nsub - 1, -1, -1) if reverse else range(nsub)
                for s in order:
                    pr = pltpu.roll(cr, shift, 0)
                    pi = pltpu.roll(ci, shift, 0)
                    cr = ar[c] * pr - ai[c] * pi + br
                    ci = ar[c] * pi + ai[c] * pr + bi
                    out_r = cr if out_r is None else jnp.where(grp == s, cr, out_r)
                    out_i = ci if out_i is None else jnp.where(grp == s, ci, out_i)
                x_s[pl.ds(r0, 8), re_sl] = out_r
                x_s[pl.ds(r0, 8), im_sl] = out_i
                new += [out_r, out_i]
            return tuple(new)

        init = tuple(st_s[q, cc] for cc in range(2 * ncc))
        fin = lax.fori_loop(0, ntile, tile_step, init)
        for cc in range(2 * ncc):
            st_s[q, cc] = fin[cc]
        y_ref[:, S5_CB * q:S5_CB * (q + 1)] = jnp.dot(x_s[...].astype(bf16), wc_ref[q],
                                                      preferred_element_type=f32)


def _s5_scan(u_tb, wb, wc, a, nb, tpb, reverse):
    rows = TM * nb
    if reverse:
        imap = lambda c: (jnp.where(c == 0, 0, tpb - c), 0)
    else:
        imap = lambda c: (c, 0)
    full = lambda shape: pl.BlockSpec(shape, lambda c: (0,) * len(shape))
    return pl.pallas_call(
        functools.partial(_s5_kernel, nb=nb, reverse=reverse),
        grid=(tpb,),
        in_specs=[pl.BlockSpec((rows, BRANCH_W), imap),
                  full((S5_NQ, S5_CB, 2 * S5_BS)), full((S5_NQ, 2 * S5_BS, S5_CB)), full((S5_NQ, 2, S5_BS))],
        out_specs=pl.BlockSpec((rows, BRANCH_W), imap),
        out_shape=jax.ShapeDtypeStruct(u_tb.shape, f32),
        scratch_shapes=[pltpu.VMEM((rows, 2 * S5_BS), f32),
                        pltpu.VMEM((S5_NQ, 2 * S5_BS // LANES, 8, LANES), f32)],
        compiler_params=_cparams(("arbitrary",)),
        name="s5_bwd" if reverse else "s5_fwd",
    )(u_tb, wb, wc, a)


def _s5_params(lam_re, lam_im, log_dt, b_re, b_im, c_re, c_im):
    dt = jnp.exp(log_dt)[:, None]
    mag = jnp.exp(lam_re * dt)
    ar = mag * jnp.cos(lam_im * dt)
    ai = mag * jnp.sin(lam_im * dt)
    den = lam_re * lam_re + lam_im * lam_im
    cr = ((ar - 1.0) * lam_re + ai * lam_im) / den
    ci = (ai * lam_re - (ar - 1.0) * lam_im) / den
    bbr = cr[..., None] * b_re - ci[..., None] * b_im
    bbi = cr[..., None] * b_im + ci[..., None] * b_re
    gl = S5_CB // S5_GROUP
    eye = jnp.eye(gl, dtype=f32)

    def wb_half(bb):
        x = bb.reshape(S5_NQ, gl, S5_STATE, S5_GROUP)
        x = jnp.einsum('qgpc,gh->qgchp', x, eye)
        return x.reshape(S5_NQ, S5_CB, S5_BS)

    def wc_half(cc):
        x = cc.reshape(S5_NQ, gl, S5_GROUP, S5_STATE)
        x = jnp.einsum('qgcp,gh->qhpgc', x, eye)
        return x.reshape(S5_NQ, S5_BS, S5_CB)

    wb = jnp.concatenate([wb_half(bbr), wb_half(bbi)], axis=2).astype(bf16)
    wc = jnp.concatenate([wc_half(c_re), wc_half(-c_im)], axis=1).astype(bf16)
    a = jnp.stack([ar.reshape(S5_NQ, S5_BS), ai.reshape(S5_NQ, S5_BS)], axis=1)
    return wb, wc, a


def _gelu_tanh(x):
    return 0.5 * x * (1.0 + jnp.tanh(math.sqrt(2.0 / math.pi) * (x + 0.044715 * x * x * x)))


def _merge_kernel(x_ref, mod_ref, g2n_ref, yr_ref, bonus_ref, gg_ref, na_ref, y5f_ref, y5b_ref, u_ref,
                  ga_ref, gn_ref, gs_ref, lnv_ref, e_ref, gluw_ref, wbr_ref, wout_ref, rw_ref, rb_ref,
                  xo_ref, h2_ref, idx_ref, tw_ref):
    mod = mod_ref[0]
    lnv = lnv_ref[...]
    e = e_ref[...]
    y = yr_ref[...]
    mu = _segsum(y, e) * (1.0 / HEAD)
    dlt = y - mu
    var = _segsum(dlt * dlt, e) * (1.0 / HEAD)
    a_out = (dlt * lax.rsqrt(var + RWKV_GN_EPS) * lnv[0:1] + lnv[1:2] + bonus_ref[...]) * gg_ref[...]
    ys = _gelu_tanh(y5f_ref[...] + y5b_ref[...] + u_ref[...] * lnv[2:3])
    glu = jnp.dot(ys.astype(bf16), gluw_ref[...], preferred_element_type=f32) + lnv[3:4]
    s_out = ys * _sigmoid(glu)
    mix = None
    for o, gate_ref, jdx in ((a_out, ga_ref, 0), (na_ref[...], gn_ref, 1), (s_out, gs_ref, 2)):
        t = _sigmoid(gate_ref[...]) * jnp.dot(o.astype(bf16), wbr_ref[jdx], preferred_element_type=f32)
        mix = t if mix is None else mix + t
    ol = jnp.dot(mix.astype(bf16), wout_ref[...], preferred_element_type=f32)
    xn = x_ref[...] + mod[2:3] * ol
    xo_ref[...] = xn
    h2 = _rms_mod(xn, g2n_ref[...], mod[4:5], mod[3:4])
    h2_ref[...] = h2
    hi = h2.astype(bf16)
    lo = (h2 - hi.astype(f32)).astype(bf16)
    logits = (jnp.dot(hi, rw_ref[0], preferred_element_type=f32) + jnp.dot(lo, rw_ref[0], preferred_element_type=f32)
              + jnp.dot(hi, rw_ref[1], preferred_element_type=f32) + rb_ref[...])
    lane = lax.broadcasted_iota(jnp.int32, logits.shape, 1)
    logits = jnp.where(lane < N_EXPERTS, logits, -jnp.inf)
    idx_acc = jnp.zeros(logits.shape, jnp.int32)
    val_acc = jnp.full(logits.shape, -jnp.inf, f32)
    top = None
    for j in range(TOP_K):
        mx = logits.max(axis=-1, keepdims=True)
        sel = jnp.min(jnp.where(logits == mx, lane, LANES), axis=-1, keepdims=True)
        idx_acc = jnp.where(lane == j, sel, idx_acc)
        val_acc = jnp.where(lane == j, mx, val_acc)
        logits = jnp.where(lane == sel, -jnp.inf, logits)
        top = mx if top is None else top
    ex = jnp.exp(val_acc - top)
    idx_ref[...] = idx_acc
    tw_ref[...] = ex / ex.sum(axis=-1, keepdims=True)


def _merge(x2, modt, g2n, yr, bonus, gg, na, y5f, y5b, p, lnv, e, gluw, wbr, wout, rw, rb, tpb):
    m, d = x2.shape
    bw = BRANCH_W
    tok = lambda w: pl.BlockSpec((TM, w), lambda i: (i, 0))
    full = lambda shape: pl.BlockSpec(shape, lambda i: (0,) * len(shape))
    gate = lambda jdx: pl.BlockSpec((TM, d), lambda i: (i, CB_GATE + jdx))
    return pl.pallas_call(
        _merge_kernel,
        grid=(m // TM,),
        in_specs=[tok(d), pl.BlockSpec((1, 8, d), lambda i: (_seg_of_tile(i, tpb), 0, 0)), full((1, d)),
                  tok(bw), tok(bw), tok(bw), tok(bw), tok(bw), tok(bw),
                  pl.BlockSpec((TM, bw), lambda i: (i, CB_U)), gate(0), gate(1), gate(2),
                  full((8, bw)), full((bw, bw)), full((bw, bw)), full((3, bw, d)), full((d, d)),
                  full((2, d, LANES)), full((1, LANES))],
        out_specs=[tok(d), tok(d), tok(LANES), tok(LANES)],
        out_shape=[jax.ShapeDtypeStruct((m, d), f32), jax.ShapeDtypeStruct((m, d), f32),
                   jax.ShapeDtypeStruct((m, LANES), jnp.int32), jax.ShapeDtypeStruct((m, LANES), f32)],
        compiler_params=_cparams(("parallel",)),
        name="merge_router",
    )(x2, modt, g2n.reshape(1, d), yr, bonus, gg, na, y5f, y5b, p, p, p, p, lnv, e, gluw, wbr, wout, rw, rb)


def _start_row_gather(idx_vmem, idx_smem, src_hbm, dst, sem_i, sem_g, n):
    cp = pltpu.make_async_copy(idx_vmem, idx_smem, sem_i)
    cp.start()
    cp.wait()

    def issue(r, carry):
        pltpu.make_async_copy(src_hbm.at[idx_smem[0, r]], dst.at[r], sem_g).start()
        return carry
    lax.fori_loop(0, n, issue, 0, unroll=8)


def _wait_row_gather(src_hbm, dst, sem_g, n):
    pltpu.make_async_copy(src_hbm.at[pl.ds(0, n)], dst, sem_g).wait()


def _expert_kernel(be_ref, nv_ref, tok_ref, tokn_ref, h_hbm, guw_ref, gub_ref, dnw_ref, dnb_ref, y_ref,
                   idx_smem, xb, sem_i, sem_g):
    i = pl.program_id(0)
    nv = nv_ref[0]
    slot = i % 2

    @pl.when(jnp.logical_and(i == 0, nv > 0))
    def _():
        _start_row_gather(tok_ref.at[0], idx_smem, h_hbm, xb.at[0], sem_i, sem_g.at[0], MOE_BM)

    @pl.when(i + 1 < nv)
    def _():
        _start_row_gather(tokn_ref.at[0], idx_smem, h_hbm, xb.at[1 - slot], sem_i, sem_g.at[1 - slot], MOE_BM)

    @pl.when(i < nv)
    def _():
        _wait_row_gather(h_hbm, xb.at[slot], sem_g.at[slot], MOE_BM)
        x = xb[slot].astype(bf16)
        gu = jnp.dot(x, guw_ref[0], preferred_element_type=f32) + gub_ref[0]
        glu = jnp.minimum(gu[:, :D_MODEL], SWIGLU_LIMIT)
        lin = jnp.clip(gu[:, D_MODEL:], -SWIGLU_LIMIT, SWIGLU_LIMIT)
        act = glu * _sigmoid(SWIGLU_ALPHA * glu) * (lin + 1.0)
        y_ref[...] = jnp.dot(act.astype(bf16), dnw_ref[0], preferred_element_type=f32) + dnb_ref[0]

    @pl.when(i >= nv_ref[0])
    def _():
        y_ref[...] = jnp.zeros_like(y_ref)


def _experts(block_expert, nvalid, slot_tok, h_rows, guw, gub, dnw, dnb):
    n_blocks = block_expert.shape[0]
    d = D_MODEL
    grid_spec = pltpu.PrefetchScalarGridSpec(
        num_scalar_prefetch=2,
        grid=(n_blocks,),
        in_specs=[pl.BlockSpec((1, 1, MOE_BM), lambda i, be, nv: (i, 0, 0)),
                  pl.BlockSpec((1, 1, MOE_BM), lambda i, be, nv: (jnp.minimum(i + 1, n_blocks - 1), 0, 0)),
                  pl.BlockSpec(memory_space=pl.ANY),
                  pl.BlockSpec((1, d, 2 * d), lambda i, be, nv: (be[i], 0, 0)),
                  pl.BlockSpec((1, 1, 2 * d), lambda i, be, nv: (be[i], 0, 0)),
                  pl.BlockSpec((1, d, d), lambda i, be, nv: (be[i], 0, 0)),
                  pl.BlockSpec((1, 1, d), lambda i, be, nv: (be[i], 0, 0))],
        out_specs=pl.BlockSpec((MOE_BM, d), lambda i, be, nv: (i, 0)),
        scratch_shapes=[pltpu.SMEM((1, MOE_BM), jnp.int32), pltpu.VMEM((2, MOE_BM, d), f32),
                        pltpu.SemaphoreType.DMA, pltpu.SemaphoreType.DMA((2,))],
    )
    tok3 = slot_tok.reshape(n_blocks, 1, MOE_BM)
    return pl.pallas_call(
        _expert_kernel,
        grid_spec=grid_spec,
        out_shape=jax.ShapeDtypeStruct((n_blocks * MOE_BM, d), f32),
        compiler_params=_cparams(("arbitrary",), disable_bounds_checks=True),
        name="moe_experts",
    )(block_expert, nvalid, tok3, tok3, h_rows, guw,
      gub.reshape(N_EXPERTS, 1, 2 * d), dnw, dnb.reshape(N_EXPERTS, 1, d))


def _combine_kernel(dest_ref, x_ref, mod_ref, tw_ref, fg_ref, yb_hbm, o_ref, idx_smem, buf, sem_i, sem_g, *, final):
    _start_row_gather(dest_ref.at[0], idx_smem, yb_hbm, buf, sem_i, sem_g, TOP_K * TM)
    _wait_row_gather(yb_hbm, buf, sem_g, TOP_K * TM)
    tw = tw_ref[...]
    y = None
    for j in range(TOP_K):
        t = tw[:, j:j + 1] * buf[TM * j:TM * (j + 1), :]
        y = t if y is None else y + t
    xn = x_ref[...] + mod_ref[0][5:6] * y
    if final:
        xn = xn * lax.rsqrt(jnp.mean(xn * xn, axis=-1, keepdims=True) + NORM_EPS) * fg_ref[...]
    o_ref[...] = xn


def _combine(dest_t, x2, modt, tw, fg, yb, tpb, final):
    m, d = x2.shape
    return pl.pallas_call(
        functools.partial(_combine_kernel, final=final),
        grid=(m // TM,),
        in_specs=[pl.BlockSpec((1, 1, TOP_K * TM), lambda i: (i, 0, 0)),
                  pl.BlockSpec((TM, d), lambda i: (i, 0)),
                  pl.BlockSpec((1, 8, d), lambda i: (_seg_of_tile(i, tpb), 0, 0)),
                  pl.BlockSpec((TM, LANES), lambda i: (i, 0)),
                  pl.BlockSpec((1, d), lambda i: (0, 0)),
                  pl.BlockSpec(memory_space=pl.ANY)],
        out_specs=pl.BlockSpec((TM, d), lambda i: (i, 0)),
        out_shape=jax.ShapeDtypeStruct((m, d), f32),
        scratch_shapes=[pltpu.SMEM((1, TOP_K * TM), jnp.int32), pltpu.VMEM((TOP_K * TM, d), f32),
                        pltpu.SemaphoreType.DMA, pltpu.SemaphoreType.DMA],
        compiler_params=_cparams(("arbitrary",), disable_bounds_checks=True),
        name="moe_combine",
    )(dest_t, x2, modt, tw, fg.reshape(1, d), yb)


def _dispatch(idx4, m):
    n_assign = m * TOP_K
    e_flat = idx4.reshape(-1)
    onehot = (e_flat[:, None] == jnp.arange(N_EXPERTS, dtype=jnp.int32)[None, :]).astype(jnp.int32)
    csum = jnp.cumsum(onehot, axis=0)
    rank = jnp.take_along_axis(csum, e_flat[:, None], axis=1)[:, 0] - 1
    counts = csum[-1]
    padded = (counts + MOE_BM - 1) // MOE_BM * MOE_BM
    pad_end = jnp.cumsum(padded)
    pad_start = pad_end - padded
    dest = (pad_start[e_flat] + rank).astype(jnp.int32)
    n_blocks = -(-n_assign // MOE_BM) + N_EXPERTS
    slot_tok = jnp.zeros((n_blocks * MOE_BM,), jnp.int32).at[dest].set(
        jnp.arange(n_assign, dtype=jnp.int32) // TOP_K)
    block_expert = jnp.minimum(
        jnp.searchsorted(pad_end, jnp.arange(n_blocks, dtype=jnp.int32) * MOE_BM, side='right'),
        N_EXPERTS - 1).astype(jnp.int32)
    nvalid = (pad_end[-1:] // MOE_BM).astype(jnp.int32)
    return dest, slot_tok, block_expert, nvalid


def _layer(x2, modt, lp, b, tpb, final_g, last):
    m = x2.shape[0]
    ttot = m // b
    p = _proj_in(x2, lp['norm1_g'], modt, lp['w7'], tpb)

    (r, v, kk, d0, d1, b0, b1, k0, k1, gg, bonus) = _rwkv_prep(
        p, lp['mu'], lp['vec'], lp['w2p'], lp['a2p'], lp['g2'], lp['e'], tpb)
    kk_rows = _to_scan_rows((kk, kk), b)
    kkn = jnp.concatenate([kk_rows[1:], jnp.zeros_like(kk_rows[:1])], axis=0)
    y_scan = _rwkv_scan(_to_scan_rows((d0, d1), b), _to_scan_rows((b0, b1), b),
                        _to_scan_rows((k0, k1), b), kkn, _to_scan_rows((r, r), b), _to_scan_v(v, b))
    yr = _from_scan_y(y_scan, b)

    na = _na(p, lp['na_bias'], b, tpb)

    u = p[:, CB_U * BRANCH_W:(CB_U + 1) * BRANCH_W]
    u_tb = jnp.transpose(u.reshape(b, ttot, BRANCH_W), (1, 0, 2)).reshape(ttot * b, BRANCH_W)
    to_bt = lambda y: jnp.transpose(y.reshape(ttot, b, BRANCH_W), (1, 0, 2)).reshape(m, BRANCH_W)
    y5f = to_bt(_s5_scan(u_tb, *lp['s5'][0], b, tpb, False))
    y5b = to_bt(_s5_scan(u_tb, *lp['s5'][1], b, tpb, True))

    xn, h2, idx, tw = _merge(x2, modt, lp['norm2_g'], yr, bonus, gg, na, y5f, y5b, p, lp['lnv'], lp['e'],
                             lp['gluw'], lp['wbr'], lp['wout'], lp['rw'], lp['rb'], tpb)

    dest, slot_tok, block_expert, nvalid = _dispatch(idx[:, :TOP_K], m)
    yb = _experts(block_expert, nvalid, slot_tok, h2, lp['guw'], lp['gub'], lp['dnw'], lp['dnb'])
    dest_t = jnp.transpose(dest.reshape(m // TM, TM, TOP_K), (0, 2, 1)).reshape(m // TM, 1, TOP_K * TM)
    return _combine(dest_t, xn, modt, tw, final_g, yb, tpb, last)


def _pad_rows(w, before, total):
    return jnp.pad(w, ((0, 0), (before, total - before - w.shape[1]), (0, 0)))


def kernel(x, c, ctx, c_ctx, ada_w, ada_b, norm1_g, norm2_g, w_in, rwkv_mu_prev, rwkv_mu_next, rwkv_w0, rwkv_w2, rwkv_a0, rwkv_a2, rwkv_g2, rwkv_k_k, rwkv_k_a, rwkv_r_k, rwkv_ln_w, rwkv_ln_b, na_rpb, s5_lambda_re, s5_lambda_im, s5_log_dt, s5_b_re, s5_b_im, s5_c_re, s5_c_im, s5_d, s5_glu_w, s5_glu_b, w_branch, w_out, router_w, router_b, expert_gu_w, expert_gu_b, expert_dn_w, expert_dn_b, final_g):
    b, s, d = x.shape
    l = ctx.shape[1]
    depth = ada_w.shape[0]
    assert d == D_MODEL and l == TM and s % TM == 0 and (s // GRID_W) >= 2 * (TM // GRID_W) and 8 % b == 0
    ttot = l + s
    tpb = ttot // TM
    m = b * ttot

    c8 = jnp.zeros((8, d), f32).at[:b].set(c).at[b].set(c_ctx)
    mods = _ada(c8, ada_w, ada_b).reshape(depth, 8, 6, d)
    eye_h = jnp.kron(jnp.eye(HEADS, dtype=f32), jnp.ones((HEAD, HEAD), f32)).astype(bf16)
    zeros_bw = jnp.zeros((BRANCH_W,), f32)

    x2 = jnp.concatenate([ctx, x], axis=1).reshape(m, d)
    for i in range(depth):
        mi = mods[i]
        modt = jnp.stack([jnp.broadcast_to(mi[b], (b, 6, d)), mi[:b]], axis=1).reshape(2 * b, 6, d)
        modt = jnp.pad(modt, ((0, 0), (0, 2), (0, 0)))
        w = w_in[i]
        w7 = jnp.concatenate([w[:, :RWKV_COLS], jnp.zeros((d, COL_PAD), f32), w[:, RWKV_COLS:]], axis=1).astype(bf16)
        rw = jnp.pad(router_w[i], ((0, 0), (0, LANES - N_EXPERTS)))
        rw_hi = rw.astype(bf16)
        rw_lo = (rw - rw_hi.astype(f32)).astype(bf16)
        lp = {
            'norm1_g': norm1_g[i], 'norm2_g': norm2_g[i], 'w7': w7,
            'mu': jnp.stack([rwkv_mu_prev[i], rwkv_mu_next[i]], 0),
            'vec': jnp.stack([rwkv_k_k[i], rwkv_k_a[i], rwkv_r_k[i].reshape(-1), rwkv_w0[i, 0], rwkv_w0[i, 1],
                              rwkv_a0[i, 0], rwkv_a0[i, 1], zeros_bw], 0),
            'w2p': _pad_rows(rwkv_w2[i], 0, LANES).astype(bf16),
            'a2p': _pad_rows(rwkv_a2[i], LORA_W, LANES).astype(bf16),
            'g2': rwkv_g2[i].astype(bf16), 'e': eye_h,
            'na_bias': _na_bias_tables(na_rpb[i]),
            's5': [_s5_params(s5_lambda_re[i, dd], s5_lambda_im[i, dd], s5_log_dt[i, dd], s5_b_re[i, dd],
                              s5_b_im[i, dd], s5_c_re[i, dd], s5_c_im[i, dd]) for dd in range(2)],
            'lnv': jnp.stack([rwkv_ln_w[i], rwkv_ln_b[i], s5_d[i], s5_glu_b[i]] + [zeros_bw] * 4, 0),
            'gluw': s5_glu_w[i].astype(bf16), 'wbr': w_branch[i].astype(bf16), 'wout': w_out[i].astype(bf16),
            'rw': jnp.stack([rw_hi, rw_lo], 0), 'rb': jnp.pad(router_b[i], (0, LANES - N_EXPERTS)).reshape(1, LANES),
            'guw': expert_gu_w[i].astype(bf16), 'gub': expert_gu_b[i],
            'dnw': expert_dn_w[i].astype(bf16), 'dnb': expert_dn_b[i],
        }
        x2 = _layer(x2, modt, lp, b, tpb, final_g, i == depth - 1)
    return x2.reshape(b, ttot, d)[:, l:]
```

```python
import functools
import math

import numpy as np
import jax
import jax.numpy as jnp
from jax import lax
from jax.experimental import pallas as pl
from jax.experimental.pallas import tpu as pltpu

f32 = jnp.float32
bf16 = jnp.bfloat16

D_MODEL = 1024
BRANCH_W = 512
GRID_W = 64
HEADS = 8
HEAD = 64
LORA_W = 64
LORA_A = 64
LORA_G = 128
RWKV_COLS = 3 * BRANCH_W + LORA_W + LORA_A + LORA_G
RWKV_GN_EPS = 64e-5
NA_KH = 8
NA_KW = 16
S5_GROUP = 16
S5_GROUPS = BRANCH_W // S5_GROUP
S5_STATE = 64
N_EXPERTS = 32
TOP_K = 4
SWIGLU_LIMIT = 7.0
SWIGLU_ALPHA = 1.702
NORM_EPS = 1e-6

TM = 256
LANES = 128
COL_PAD = 256
P_COLS = RWKV_COLS + COL_PAD + 3 * BRANCH_W + BRANCH_W + 3 * D_MODEL
CB_Q, CB_K, CB_V, CB_U = 4, 5, 6, 7
CB_GATE = 4
IN_TN = 1792
NA_NEG = -1e30
S5_CB = 128
S5_NQ = BRANCH_W // S5_CB
S5_BS = (S5_CB // S5_GROUP) * S5_STATE
MOE_BM = 256
VMEM_LIMIT = 56 * 1024 * 1024


def _cparams(sem, **kw):
    return pltpu.CompilerParams(dimension_semantics=sem, vmem_limit_bytes=VMEM_LIMIT, **kw)


def _segsum(x, e):
    hi = x.astype(bf16)
    lo = (x - hi.astype(f32)).astype(bf16)
    return jnp.dot(hi, e, preferred_element_type=f32) + jnp.dot(lo, e, preferred_element_type=f32)


def _sigmoid(x):
    return 1.0 / (1.0 + jnp.exp(-x))


def _softplus(x):
    return jnp.maximum(x, 0.0) + jnp.log(1.0 + jnp.exp(-jnp.abs(x)))


def _rms_mod(x, g, scale, shift):
    y = x * lax.rsqrt(jnp.mean(x * x, axis=-1, keepdims=True) + NORM_EPS)
    return y * g * (1.0 + scale) + shift


def _ada_kernel(c_ref, w_ref, b_ref, o_ref):
    c = c_ref[...]
    act = (c * _sigmoid(c)).astype(bf16)
    o_ref[0] = jnp.dot(act, w_ref[0].astype(bf16), preferred_element_type=f32) + b_ref[0]


def _ada(c8, ada_w, ada_b):
    depth, d, n = ada_w.shape
    tn = 1536
    return pl.pallas_call(
        _ada_kernel,
        grid=(depth, n // tn),
        in_specs=[pl.BlockSpec((8, d), lambda l, j: (0, 0)),
                  pl.BlockSpec((1, d, tn), lambda l, j: (l, 0, j)),
                  pl.BlockSpec((1, 1, tn), lambda l, j: (l, 0, j))],
        out_specs=pl.BlockSpec((1, 8, tn), lambda l, j: (l, 0, j)),
        out_shape=jax.ShapeDtypeStruct((depth, 8, n), f32),
        compiler_params=_cparams(("parallel", "parallel")),
        name="ada_mod",
    )(c8, ada_w, ada_b.reshape(depth, 1, n))


def _in_kernel(x_ref, g_ref, mod_ref, w_ref, o_ref):
    mod = mod_ref[0]
    h = _rms_mod(x_ref[...], g_ref[...], mod[1:2], mod[0:1]).astype(bf16)
    o_ref[...] = jnp.dot(h, w_ref[...], preferred_element_type=f32)


def _seg_of_tile(i, tpb):
    return 2 * (i // tpb) + jnp.minimum(i % tpb, 1)


def _proj_in(x2, g, modt, w7, tpb):
    m, d = x2.shape
    return pl.pallas_call(
        _in_kernel,
        grid=(P_COLS // IN_TN, m // TM),
        in_specs=[pl.BlockSpec((TM, d), lambda j, i: (i, 0)),
                  pl.BlockSpec((1, d), lambda j, i: (0, 0)),
                  pl.BlockSpec((1, 8, d), lambda j, i: (_seg_of_tile(i, tpb), 0, 0)),
                  pl.BlockSpec((d, IN_TN), lambda j, i: (0, j))],
        out_specs=pl.BlockSpec((TM, IN_TN), lambda j, i: (i, j)),
        out_shape=jax.ShapeDtypeStruct((m, P_COLS), f32),
        compiler_params=_cparams(("parallel", "parallel")),
        name="proj_in",
    )(x2, g.reshape(1, d), modt, w7)


def _rwkv_prep_kernel(p_ref, pp_ref, pn_ref, mu_ref, vec_ref, w2_ref, a2_ref, g2_ref, e_ref,
                      r_o, v_o, kk_o, d0_o, d1_o, b0_o, b1_o, k0_o, k1_o, g_o, bonus_o, *, tpb):
    j = pl.program_id(0) % tpb
    prev_ok = jnp.logical_and(j != 0, j != 1)
    next_ok = jnp.logical_and(j != 0, j != tpb - 1)
    p = p_ref[...]
    row = lax.broadcasted_iota(jnp.int32, p.shape, 0)
    hp = jnp.where(prev_ok, pp_ref[7:8, :], 0.0)
    hn = jnp.where(next_ok, pn_ref[0:1, :], 0.0)
    prev = jnp.where(row == 0, hp, pltpu.roll(p, 1, 0))
    nxt = jnp.where(row == TM - 1, hn, pltpu.roll(p, TM - 1, 0))
    mu = mu_ref[...]
    z = p + mu[0:1] * (prev - p) + mu[1:2] * (nxt - p)

    r = z[:, 0:BRANCH_W]
    k = z[:, BRANCH_W:2 * BRANCH_W]
    v = z[:, 2 * BRANCH_W:3 * BRANCH_W]
    wa = z[:, 3 * BRANCH_W:3 * BRANCH_W + LORA_W + LORA_A]
    gd = z[:, 3 * BRANCH_W + LORA_W + LORA_A:RWKV_COLS]
    vec = vec_ref[...]
    e = e_ref[...]
    kk = k * vec[0:1]
    kk = kk / jnp.maximum(jnp.sqrt(_segsum(kk * kk, e)), 1e-12)
    g_o[...] = jnp.dot(_sigmoid(gd).astype(bf16), g2_ref[...], preferred_element_type=f32)
    tw = jnp.tanh(wa).astype(bf16)
    wab = wa.astype(bf16)
    ksum = None
    for d, (dec_o, b_o, k_o) in enumerate(((d0_o, b0_o, k0_o), (d1_o, b1_o, k1_o))):
        wl = vec[3 + d:4 + d] + jnp.dot(tw, w2_ref[d], preferred_element_type=f32)
        w = -_softplus(-wl) - 0.5
        dec_o[...] = jnp.exp(-jnp.exp(w)).T
        a = _sigmoid(vec[5 + d:6 + d] + jnp.dot(wab, a2_ref[d], preferred_element_type=f32))
        kd = k * (1.0 + (a - 1.0) * vec[1:2])
        b_o[...] = (kk * a).T
        k_o[...] = kd.T
        ksum = kd if ksum is None else ksum + kd
    r_o[...] = r.T
    v_o[...] = v
    kk_o[...] = kk.T
    bonus_o[...] = _segsum(r * ksum * vec[2:3], e) * v


def _rwkv_prep(p, mu, vec, w2p, a2p, g2, e, tpb):
    m = p.shape[0]
    bw = BRANCH_W
    full = lambda shape: pl.BlockSpec(shape, lambda i: (0,) * len(shape))
    nat = pl.BlockSpec((TM, bw), lambda i: (i, 0))
    nat_shape = jax.ShapeDtypeStruct((m, bw), f32)
    tr = pl.BlockSpec((None, bw, TM), lambda i: (i // tpb, 0, i % tpb))
    tr_shape = jax.ShapeDtypeStruct((m // (tpb * TM), bw, tpb * TM), f32)
    return pl.pallas_call(
        functools.partial(_rwkv_prep_kernel, tpb=tpb),
        grid=(m // TM,),
        in_specs=[pl.BlockSpec((TM, RWKV_COLS), lambda i: (i, 0)),
                  pl.BlockSpec((8, RWKV_COLS), lambda i: (jnp.maximum(i * (TM // 8) - 1, 0), 0)),
                  pl.BlockSpec((8, RWKV_COLS), lambda i: (jnp.minimum((i + 1) * (TM // 8), m // 8 - 1), 0)),
                  full((2, RWKV_COLS)), full((8, bw)), full((2, LANES, bw)), full((2, LANES, bw)),
                  full((LORA_G, bw)), full((bw, bw))],
        out_specs=[tr, nat, tr] + [tr] * 6 + [nat, nat],
        out_shape=[tr_shape, nat_shape, tr_shape] + [tr_shape] * 6 + [nat_shape, nat_shape],
        compiler_params=_cparams(("parallel",)),
        name="rwkv_prep",
    )(p, p, p, mu, vec, w2p, a2p, g2, e)


SC = 128
KG = HEAD // 4
N_COEF = 5
N_ACC = 2


def _lane_allsum(x, gw):
    x = x + pltpu.roll(x, gw, 1)
    return x + pltpu.roll(x, 2 * gw, 1)


def _rwkv_scan_kernel(wf, bf, kf, kkf, rf, wb, bb, kb, kkb, rb, vf_ref, vb_ref, yf_ref, yb_ref,
                      a4, sf, sb, ypart, *, nb):
    gw = HEADS * nb
    lanes = 4 * gw

    @pl.when(pl.program_id(0) == 0)
    def _():
        sf[...] = jnp.zeros_like(sf)
        sb[...] = jnp.zeros_like(sb)

    srcs = (wf, bf, kf, kkf, rf, wb, bb, kb, kkb, rb)
    for kg in range(KG):
        for vi, src in enumerate(srcs):
            pieces = [src[b, pl.ds(4 * kg + k4, HEADS, stride=HEAD), :] for k4 in range(4) for b in range(nb)]
            a4[vi, kg] = jnp.concatenate(pieces, axis=0).T

    hv = HEAD // 2
    grp = lax.broadcasted_iota(jnp.int32, (hv, lanes), 1) // gw
    zero = jnp.zeros((hv, lanes), f32)

    passes = [(dirn, vh) for dirn in range(2) for vh in range(2)]
    s_refs, v_refs, y_refs = (sf, sb), (vf_ref, vb_ref), (yf_ref, yb_ref)
    time_of = lambda dirn, j: j if dirn == 0 else SC - 1 - j
    vslice = lambda vh: slice(hv * vh, hv * (vh + 1))

    def tree_sum(terms):
        while len(terms) > 1:
            terms = [a + b for a, b in zip(terms[::2], terms[1::2])]
        return terms[0]

    def spread(tq, carry):
        for dirn, vh in passes:
            vs = vslice(vh)
            for g in range(4):
                ypart[dirn, tq * 4 + g, vs, :] = _lane_allsum(jnp.where(grp == g, v_refs[dirn][tq, vs, :], 0.0), gw)
        return carry

    lax.fori_loop(0, SC // 4, spread, 0)

    def first_sa(dirn, vh):
        t0 = time_of(dirn, 0)
        return tree_sum([a4[N_COEF * dirn + 3, kg, t0:t0 + 1, :] * s_refs[dirn][kg, vslice(vh), :]
                         for kg in range(KG)])

    def one_pass(dirn, vh, j, sa):
        s_ref = s_refs[dirn]
        vs = vslice(vh)
        t = time_of(dirn, j)
        tn = jnp.clip(time_of(dirn, j + 1), 0, SC - 1)
        row = lambda vi, kg, tt=t: a4[N_COEF * dirn + vi, kg, pl.ds(tt, 1), :]
        vt = ypart[dirn, t, vs, :]
        ys, ns = [zero] * N_ACC, [zero] * N_ACC
        for kg in range(KG):
            s_new = row(0, kg) * s_ref[kg, vs, :] - row(1, kg) * sa + row(2, kg) * vt
            s_ref[kg, vs, :] = s_new
            ys[kg % N_ACC] = ys[kg % N_ACC] + row(4, kg) * s_new
            ns[kg % N_ACC] = ns[kg % N_ACC] + row(3, kg, tn) * s_new
        ypart[dirn, t, vs, :] = tree_sum(ys)
        return tree_sum(ns)

    def step(j, carry):
        sa_f, part_b = carry[:2], carry[2:]
        sa_b = [_lane_allsum(p, gw) for p in part_b]
        new_f = [_lane_allsum(one_pass(0, vh, j, sa_f[vh]), gw) for vh in range(2)]
        new_b = [one_pass(1, vh, j, sa_b[vh]) for vh in range(2)]
        return tuple(new_f + new_b)

    init = [_lane_allsum(first_sa(0, vh), gw) for vh in range(2)] + [first_sa(1, vh) for vh in range(2)]
    lax.fori_loop(0, SC, step, tuple(init))

    def pack(tq, carry):
        for dirn, vh in passes:
            vs = vslice(vh)
            tile = zero
            for g in range(4):
                tile = jnp.where(grp == g, _lane_allsum(ypart[dirn, tq * 4 + g, vs, :], gw), tile)
            y_refs[dirn][tq, vs, :] = tile
        return carry

    lax.fori_loop(0, SC // 4, pack, 0)


def _rwkv_scan(xt, vt4, nb):
    ttot = xt['r'].shape[2]
    nch = ttot // SC
    nctx = TM // SC
    lanes = 4 * HEADS * nb
    rc = lambda c: jnp.where(c < nctx, nctx - 1 - c, nctx + nch - 1 - c)
    cf = pl.BlockSpec((nb, BRANCH_W, SC), lambda c: (0, 0, c))
    cb = pl.BlockSpec((nb, BRANCH_W, SC), lambda c: (0, 0, rc(c)))
    tf = pl.BlockSpec((SC // 4, HEAD, lanes), lambda c: (c, 0, 0))
    tb = pl.BlockSpec((SC // 4, HEAD, lanes), lambda c: (rc(c), 0, 0))
    out = jax.ShapeDtypeStruct((ttot // 4, HEAD, lanes), f32)
    return pl.pallas_call(
        functools.partial(_rwkv_scan_kernel, nb=nb),
        grid=(nch,),
        in_specs=[cf] * 5 + [cb] * 5 + [tf, tb],
        out_specs=[tf, tb],
        out_shape=[out, out],
        scratch_shapes=[pltpu.VMEM((2 * N_COEF, KG, SC, lanes), f32),
                        pltpu.VMEM((KG, HEAD, lanes), f32), pltpu.VMEM((KG, HEAD, lanes), f32),
                        pltpu.VMEM((2, SC, HEAD, lanes), f32)],
        compiler_params=_cparams(("arbitrary",)),
        name="rwkv_scan",
    )(xt['d0'], xt['b0'], xt['k0'], xt['kk'], xt['r'], xt['d1'], xt['b1'], xt['k1'], xt['kk'], xt['r'], vt4, vt4)


def _to_scan_v(v, b):
    t = v.shape[0] // b
    x = v.reshape(b, t // 4, 4, HEADS, HEAD)
    return jnp.transpose(x, (1, 4, 2, 0, 3)).reshape(t // 4, HEAD, 4 * b * HEADS)


def _from_scan_y(yf4, yb4, b):
    tq = yf4.shape[0]
    x = (yf4 + yb4).reshape(tq, HEAD, 4, b, HEADS)
    return jnp.transpose(x, (3, 0, 2, 4, 1)).reshape(b * tq * 4, BRANCH_W)


def _softmax_pv(q, ks, vs, biases):
    ss = []
    for kt, bias in zip(ks, biases):
        s = lax.dot_general(q, kt, (((1,), (1,)), ((), ())), preferred_element_type=f32)
        ss.append(s if bias is None else s + bias)
    m = ss[0].max(axis=-1, keepdims=True)
    for s in ss[1:]:
        m = jnp.maximum(m, s.max(axis=-1, keepdims=True))
    den = None
    acc = None
    for s, vt in zip(ss, vs):
        pr = jnp.exp(s - m)
        d = pr.sum(axis=-1, keepdims=True)
        o = jnp.dot(pr.astype(bf16), vt, preferred_element_type=f32)
        den = d if den is None else den + d
        acc = o if acc is None else acc + o
    return acc / den


def _na_kernel(q_ref, kc_ref, vc_ref, k0_ref, k1_ref, k2_ref, v0_ref, v1_ref, v2_ref, bm_ref, o_ref):
    j = pl.program_id(1)
    scale = HEAD ** -0.5

    @pl.when(j == 0)
    def _():
        for h in range(HEADS):
            sl = slice(HEAD * h, HEAD * (h + 1))
            q = (q_ref[:, sl] * scale).astype(bf16)
            o_ref[:, sl] = _softmax_pv(q, [kc_ref[:, sl].astype(bf16)], [vc_ref[:, sl].astype(bf16)], [None])

    @pl.when(j > 0)
    def _():
        for h in range(HEADS):
            sl = slice(HEAD * h, HEAD * (h + 1))
            q = (q_ref[:, sl] * scale).astype(bf16)
            ks = [r[:, sl].astype(bf16) for r in (k0_ref, k1_ref, k2_ref, kc_ref)]
            vs = [r[:, sl].astype(bf16) for r in (v0_ref, v1_ref, v2_ref, vc_ref)]
            biases = [bm_ref[0, h, :, TM * s:TM * (s + 1)] for s in range(3)] + [None]
            o_ref[:, sl] = _softmax_pv(q, ks, vs, biases)


def _na(p, biasmask, b, tpb):
    m = p.shape[0]
    nlb = tpb - 1
    bw = BRANCH_W

    def kv(col, slot):
        def imap(bi, j):
            jj = jnp.maximum(j - 1, 0)
            return (bi * tpb + 1 + jnp.clip(jj - 1 + slot, 0, nlb - 1), col)
        return pl.BlockSpec((TM, bw), imap)

    def variant(bi, j):
        jj = jnp.maximum(j - 1, 0)
        return (jnp.where(jj == 0, 0, jnp.where(jj == nlb - 1, 2, 1)), 0, 0, 0)

    return pl.pallas_call(
        _na_kernel,
        grid=(b, tpb),
        in_specs=[pl.BlockSpec((TM, bw), lambda bi, j: (bi * tpb + j, CB_Q)),
                  pl.BlockSpec((TM, bw), lambda bi, j: (bi * tpb, CB_K)),
                  pl.BlockSpec((TM, bw), lambda bi, j: (bi * tpb, CB_V)),
                  kv(CB_K, 0), kv(CB_K, 1), kv(CB_K, 2), kv(CB_V, 0), kv(CB_V, 1), kv(CB_V, 2),
                  pl.BlockSpec((1, HEADS, TM, 3 * TM), variant)],
        out_specs=pl.BlockSpec((TM, bw), lambda bi, j: (bi * tpb + j, 0)),
        out_shape=jax.ShapeDtypeStruct((m, bw), f32),
        compiler_params=_cparams(("parallel", "arbitrary")),
        name="na_attn",
    )(p, p, p, p, p, p, p, p, p, biasmask)


def _na_bias_tables(rpb):
    qr = TM // GRID_W
    dr = np.arange(qr)
    krel = np.arange(3 * qr)
    qc = np.arange(GRID_W)
    kc = np.arange(GRID_W)
    roff = krel[None, :] - dr[:, None] + (NA_KH - 1 - qr)
    coff = kc[None, :] - qc[:, None] + NA_KW - 1
    cs = np.clip(qc - NA_KW // 2, 0, GRID_W - NA_KW)
    colvalid = (kc[None, :] >= cs[:, None]) & (kc[None, :] < cs[:, None] + NA_KW)
    rv_int = (krel[None, :] >= dr[:, None]) & (krel[None, :] < dr[:, None] + NA_KH)
    rv_first = np.broadcast_to((krel >= qr) & (krel < qr + NA_KH), rv_int.shape)
    rv_last = np.broadcast_to(krel < NA_KH, rv_int.shape)
    rowvalid = np.stack([rv_first, rv_int, rv_last], 0)
    valid = rowvalid[:, :, None, :, None] & colvalid[None, None, :, None, :]
    r1 = (roff[:, :, None] == np.arange(2 * NA_KH - 1)).astype(np.float32)
    c1 = (coff[:, :, None] == np.arange(2 * NA_KW - 1)).astype(np.float32)
    tmp = jnp.einsum('rka,hab->hrkb', r1, rpb, precision=lax.Precision.HIGHEST)
    bias = jnp.einsum('hrkb,qcb->hrqkc', tmp, c1, precision=lax.Precision.HIGHEST)
    tab = jnp.where(valid[:, None], bias[None], NA_NEG)
    return tab.reshape(3, HEADS, TM, 3 * TM).astype(f32)


def _s5_kernel(u_ref, wb_ref, wc_ref, a_ref, y_ref, x_s, st_s, *, nb, reverse):
    rows = x_s.shape[0]
    ntile = rows // 8
    nsub = 8 // nb
    ncc = S5_BS // LANES
    shift = (8 - nb) if reverse else nb

    @pl.when(pl.program_id(0) == 0)
    def _():
        st_s[...] = jnp.zeros_like(st_s)

    grp = lax.broadcasted_iota(jnp.int32, (8, LANES), 0) // nb
    for q in range(S5_NQ):
        u = u_ref[:, S5_CB * q:S5_CB * (q + 1)].astype(bf16)
        x_s[...] = jnp.dot(u, wb_ref[q], preferred_element_type=f32)
        ar = [jnp.broadcast_to(a_ref[q, 0:1, LANES * c:LANES * (c + 1)], (8, LANES)) for c in range(ncc)]
        ai = [jnp.broadcast_to(a_ref[q, 1:2, LANES * c:LANES * (c + 1)], (8, LANES)) for c in range(ncc)]

        def tile_step(i, carry):
            rt = (ntile - 1 - i) if reverse else i
            r0 = pl.multiple_of(rt * 8, 8)
            new = []
            for c in range(ncc):
                cr, ci = carry[2 * c], carry[2 * c + 1]
                re_sl = slice(LANES * c, LANES * (c + 1))
                im_sl = slice(S5_BS + LANES * c, S5_BS + LANES * (c + 1))
                br = x_s[pl.ds(r0, 8), re_sl]
                bi = x_s[pl.ds(r0, 8), im_sl]
                out_r = out_i = None
                order = range(nsub - 1, -1, -1) if reverse else range(nsub)
                for s in order:
                    pr = pltpu.roll(cr, shift, 0)
                    pi = pltpu.roll(ci, shift, 0)
                    cr = ar[c] * pr - ai[c] * pi + br
                    ci = ar[c] * pi + ai[c] * pr + bi
                    out_r = cr if out_r is None else jnp.where(grp == s, cr, out_r)
                    out_i = ci if out_i is None else jnp.where(grp == s, ci, out_i)
                x_s[pl.ds(r0, 8), re_sl] = out_r
                x_s[pl.ds(r0, 8), im_sl] = out_i
                new += [out_r, out_i]
            return tuple(new)

        init = tuple(st_s[q, cc] for cc in range(2 * ncc))
        fin = lax.fori_loop(0, ntile, tile_step, init)
        for cc in range(2 * ncc):
            st_s[q, cc] = fin[cc]
        y_ref[:, S5_CB * q:S5_CB * (q + 1)] = jnp.dot(x_s[...].astype(bf16), wc_ref[q],
                                                      preferred_element_type=f32)


def _s5_scan(u_tb, wb, wc, a, nb, tpb, reverse):
    rows = TM * nb
    if reverse:
        imap = lambda c: (jnp.where(c == 0, 0, tpb - c), 0)
    else:
        imap = lambda c: (c, 0)
    full = lambda shape: pl.BlockSpec(shape, lambda c: (0,) * len(shape))
    return pl.pallas_call(
        functools.partial(_s5_kernel, nb=nb, reverse=reverse),
        grid=(tpb,),
        in_specs=[pl.BlockSpec((rows, BRANCH_W), imap),
                  full((S5_NQ, S5_CB, 2 * S5_BS)), full((S5_NQ, 2 * S5_BS, S5_CB)), full((S5_NQ, 2, S5_BS))],
        out_specs=pl.BlockSpec((rows, BRANCH_W), imap),
        out_shape=jax.ShapeDtypeStruct(u_tb.shape, f32),
        scratch_shapes=[pltpu.VMEM((rows, 2 * S5_BS), f32),
                        pltpu.VMEM((S5_NQ, 2 * S5_BS // LANES, 8, LANES), f32)],
        compiler_params=_cparams(("arbitrary",)),
        name="s5_bwd" if reverse else "s5_fwd",
    )(u_tb, wb, wc, a)


def _s5_params(lam_re, lam_im, log_dt, b_re, b_im, c_re, c_im):
    dt = jnp.exp(log_dt)[:, None]
    mag = jnp.exp(lam_re * dt)
    ar = mag * jnp.cos(lam_im * dt)
    ai = mag * jnp.sin(lam_im * dt)
    den = lam_re * lam_re + lam_im * lam_im
    cr = ((ar - 1.0) * lam_re + ai * lam_im) / den
    ci = (ai * lam_re - (ar - 1.0) * lam_im) / den
    bbr = cr[..., None] * b_re - ci[..., None] * b_im
    bbi = cr[..., None] * b_im + ci[..., None] * b_re
    gl = S5_CB // S5_GROUP
    eye = jnp.eye(gl, dtype=f32)

    def wb_half(bb):
        x = bb.reshape(S5_NQ, gl, S5_STATE, S5_GROUP)
        x = jnp.einsum('qgpc,gh->qgchp', x, eye)
        return x.reshape(S5_NQ, S5_CB, S5_BS)

    def wc_half(cc):
        x = cc.reshape(S5_NQ, gl, S5_GROUP, S5_STATE)
        x = jnp.einsum('qgcp,gh->qhpgc', x, eye)
        return x.reshape(S5_NQ, S5_BS, S5_CB)

    wb = jnp.concatenate([wb_half(bbr), wb_half(bbi)], axis=2).astype(bf16)
    wc = jnp.concatenate([wc_half(c_re), wc_half(-c_im)], axis=1).astype(bf16)
    a = jnp.stack([ar.reshape(S5_NQ, S5_BS), ai.reshape(S5_NQ, S5_BS)], axis=1)
    return wb, wc, a


def _gelu_tanh(x):
    return 0.5 * x * (1.0 + jnp.tanh(math.sqrt(2.0 / math.pi) * (x + 0.044715 * x * x * x)))


def _merge_kernel(x_ref, mod_ref, g2n_ref, yr_ref, bonus_ref, gg_ref, na_ref, y5f_ref, y5b_ref, u_ref,
                  ga_ref, gn_ref, gs_ref, lnv_ref, e_ref, gluw_ref, wbr_ref, wout_ref, rw_ref, rb_ref,
                  xo_ref, h2_ref, idx_ref, tw_ref):
    mod = mod_ref[0]
    lnv = lnv_ref[...]
    e = e_ref[...]
    y = yr_ref[...]
    mu = _segsum(y, e) * (1.0 / HEAD)
    dlt = y - mu
    var = _segsum(dlt * dlt, e) * (1.0 / HEAD)
    a_out = (dlt * lax.rsqrt(var + RWKV_GN_EPS) * lnv[0:1] + lnv[1:2] + bonus_ref[...]) * gg_ref[...]
    ys = _gelu_tanh(y5f_ref[...] + y5b_ref[...] + u_ref[...] * lnv[2:3])
    glu = jnp.dot(ys.astype(bf16), gluw_ref[...], preferred_element_type=f32) + lnv[3:4]
    s_out = ys * _sigmoid(glu)
    mix = None
    for o, gate_ref, jdx in ((a_out, ga_ref, 0), (na_ref[...], gn_ref, 1), (s_out, gs_ref, 2)):
        t = _sigmoid(gate_ref[...]) * jnp.dot(o.astype(bf16), wbr_ref[jdx], preferred_element_type=f32)
        mix = t if mix is None else mix + t
    ol = jnp.dot(mix.astype(bf16), wout_ref[...], preferred_element_type=f32)
    xn = x_ref[...] + mod[2:3] * ol
    xo_ref[...] = xn
    h2 = _rms_mod(xn, g2n_ref[...], mod[4:5], mod[3:4])
    h2_ref[...] = h2
    hi = h2.astype(bf16)
    lo = (h2 - hi.astype(f32)).astype(bf16)
    logits = (jnp.dot(hi, rw_ref[0], preferred_element_type=f32) + jnp.dot(lo, rw_ref[0], preferred_element_type=f32)
              + jnp.dot(hi, rw_ref[1], preferred_element_type=f32) + rb_ref[...])
    lane = lax.broadcasted_iota(jnp.int32, logits.shape, 1)
    logits = jnp.where(lane < N_EXPERTS, logits, -jnp.inf)
    idx_acc = jnp.zeros(logits.shape, jnp.int32)
    val_acc = jnp.full(logits.shape, -jnp.inf, f32)
    top = None
    for j in range(TOP_K):
        mx = logits.max(axis=-1, keepdims=True)
        sel = jnp.min(jnp.where(logits == mx, lane, LANES), axis=-1, keepdims=True)
        idx_acc = jnp.where(lane == j, sel, idx_acc)
        val_acc = jnp.where(lane == j, mx, val_acc)
        logits = jnp.where(lane == sel, -jnp.inf, logits)
        top = mx if top is None else top
    ex = jnp.exp(val_acc - top)
    idx_ref[...] = idx_acc
    tw_ref[...] = ex / ex.sum(axis=-1, keepdims=True)


def _merge(x2, modt, g2n, yr, bonus, gg, na, y5f, y5b, p, lnv, e, gluw, wbr, wout, rw, rb, tpb):
    m, d = x2.shape
    bw = BRANCH_W
    tok = lambda w: pl.BlockSpec((TM, w), lambda i: (i, 0))
    full = lambda shape: pl.BlockSpec(shape, lambda i: (0,) * len(shape))
    gate = lambda jdx: pl.BlockSpec((TM, d), lambda i: (i, CB_GATE + jdx))
    return pl.pallas_call(
        _merge_kernel,
        grid=(m // TM,),
        in_specs=[tok(d), pl.BlockSpec((1, 8, d), lambda i: (_seg_of_tile(i, tpb), 0, 0)), full((1, d)),
                  tok(bw), tok(bw), tok(bw), tok(bw), tok(bw), tok(bw),
                  pl.BlockSpec((TM, bw), lambda i: (i, CB_U)), gate(0), gate(1), gate(2),
                  full((8, bw)), full((bw, bw)), full((bw, bw)), full((3, bw, d)), full((d, d)),
                  full((2, d, LANES)), full((1, LANES))],
        out_specs=[tok(d), tok(d), tok(LANES), tok(LANES)],
        out_shape=[jax.ShapeDtypeStruct((m, d), f32), jax.ShapeDtypeStruct((m, d), f32),
                   jax.ShapeDtypeStruct((m, LANES), jnp.int32), jax.ShapeDtypeStruct((m, LANES), f32)],
        compiler_params=_cparams(("parallel",)),
        name="merge_router",
    )(x2, modt, g2n.reshape(1, d), yr, bonus, gg, na, y5f, y5b, p, p, p, p, lnv, e, gluw, wbr, wout, rw, rb)


def _start_row_gather(idx_vmem, idx_smem, src_hbm, dst, sem_i, sem_g, n):
    cp = pltpu.make_async_copy(idx_vmem, idx_smem, sem_i)
    cp.start()
    cp.wait()

    def issue(r, carry):
        pltpu.make_async_copy(src_hbm.at[idx_smem[0, r]], dst.at[r], sem_g).start()
        return carry
    lax.fori_loop(0, n, issue, 0, unroll=8)


def _wait_row_gather(src_hbm, dst, sem_g, n):
    pltpu.make_async_copy(src_hbm.at[pl.ds(0, n)], dst, sem_g).wait()


def _expert_kernel(be_ref, nv_ref, tok_ref, tokn_ref, h_hbm, guw_ref, gub_ref, dnw_ref, dnb_ref, y_ref,
                   idx_smem, xb, sem_i, sem_g):
    i = pl.program_id(0)
    nv = nv_ref[0]
    slot = i % 2

    @pl.when(jnp.logical_and(i == 0, nv > 0))
    def _():
        _start_row_gather(tok_ref.at[0], idx_smem, h_hbm, xb.at[0], sem_i, sem_g.at[0], MOE_BM)

    @pl.when(i + 1 < nv)
    def _():
        _start_row_gather(tokn_ref.at[0], idx_smem, h_hbm, xb.at[1 - slot], sem_i, sem_g.at[1 - slot], MOE_BM)

    @pl.when(i < nv)
    def _():
        _wait_row_gather(h_hbm, xb.at[slot], sem_g.at[slot], MOE_BM)
        x = xb[slot].astype(bf16)
        gu = jnp.dot(x, guw_ref[0], preferred_element_type=f32) + gub_ref[0]
        glu = jnp.minimum(gu[:, :D_MODEL], SWIGLU_LIMIT)
        lin = jnp.clip(gu[:, D_MODEL:], -SWIGLU_LIMIT, SWIGLU_LIMIT)
        act = glu * _sigmoid(SWIGLU_ALPHA * glu) * (lin + 1.0)
        y_ref[...] = jnp.dot(act.astype(bf16), dnw_ref[0], preferred_element_type=f32) + dnb_ref[0]

    @pl.when(i >= nv_ref[0])
    def _():
        y_ref[...] = jnp.zeros_like(y_ref)


def _experts(block_expert, nvalid, slot_tok, h_rows, guw, gub, dnw, dnb):
    n_blocks = block_expert.shape[0]
    d = D_MODEL
    grid_spec = pltpu.PrefetchScalarGridSpec(
        num_scalar_prefetch=2,
        grid=(n_blocks,),
        in_specs=[pl.BlockSpec((1, 1, MOE_BM), lambda i, be, nv: (i, 0, 0)),
                  pl.BlockSpec((1, 1, MOE_BM), lambda i, be, nv: (jnp.minimum(i + 1, n_blocks - 1), 0, 0)),
                  pl.BlockSpec(memory_space=pl.ANY),
                  pl.BlockSpec((1, d, 2 * d), lambda i, be, nv: (be[i], 0, 0)),
                  pl.BlockSpec((1, 1, 2 * d), lambda i, be, nv: (be[i], 0, 0)),
                  pl.BlockSpec((1, d, d), lambda i, be, nv: (be[i], 0, 0)),
                  pl.BlockSpec((1, 1, d), lambda i, be, nv: (be[i], 0, 0))],
        out_specs=pl.BlockSpec((MOE_BM, d), lambda i, be, nv: (i, 0)),
        scratch_shapes=[pltpu.SMEM((1, MOE_BM), jnp.int32), pltpu.VMEM((2, MOE_BM, d), f32),
                        pltpu.SemaphoreType.DMA, pltpu.SemaphoreType.DMA((2,))],
    )
    tok3 = slot_tok.reshape(n_blocks, 1, MOE_BM)
    return pl.pallas_call(
        _expert_kernel,
        grid_spec=grid_spec,
        out_shape=jax.ShapeDtypeStruct((n_blocks * MOE_BM, d), f32),
        compiler_params=_cparams(("arbitrary",), disable_bounds_checks=True),
        name="moe_experts",
    )(block_expert, nvalid, tok3, tok3, h_rows, guw,
      gub.reshape(N_EXPERTS, 1, 2 * d), dnw, dnb.reshape(N_EXPERTS, 1, d))


def _combine_kernel(dest_ref, x_ref, mod_ref, tw_ref, fg_ref, yb_hbm, o_ref, idx_smem, buf, sem_i, sem_g, *, final):
    _start_row_gather(dest_ref.at[0], idx_smem, yb_hbm, buf, sem_i, sem_g, TOP_K * TM)
    _wait_row_gather(yb_hbm, buf, sem_g, TOP_K * TM)
    tw = tw_ref[...]
    y = None
    for j in range(TOP_K):
        t = tw[:, j:j + 1] * buf[TM * j:TM * (j + 1), :]
        y = t if y is None else y + t
    xn = x_ref[...] + mod_ref[0][5:6] * y
    if final:
        xn = xn * lax.rsqrt(jnp.mean(xn * xn, axis=-1, keepdims=True) + NORM_EPS) * fg_ref[...]
    o_ref[...] = xn


def _combine(dest_t, x2, modt, tw, fg, yb, tpb, final):
    m, d = x2.shape
    return pl.pallas_call(
        functools.partial(_combine_kernel, final=final),
        grid=(m // TM,),
        in_specs=[pl.BlockSpec((1, 1, TOP_K * TM), lambda i: (i, 0, 0)),
                  pl.BlockSpec((TM, d), lambda i: (i, 0)),
                  pl.BlockSpec((1, 8, d), lambda i: (_seg_of_tile(i, tpb), 0, 0)),
                  pl.BlockSpec((TM, LANES), lambda i: (i, 0)),
                  pl.BlockSpec((1, d), lambda i: (0, 0)),
                  pl.BlockSpec(memory_space=pl.ANY)],
        out_specs=pl.BlockSpec((TM, d), lambda i: (i, 0)),
        out_shape=jax.ShapeDtypeStruct((m, d), f32),
        scratch_shapes=[pltpu.SMEM((1, TOP_K * TM), jnp.int32), pltpu.VMEM((TOP_K * TM, d), f32),
                        pltpu.SemaphoreType.DMA, pltpu.SemaphoreType.DMA],
        compiler_params=_cparams(("arbitrary",), disable_bounds_checks=True),
        name="moe_combine",
    )(dest_t, x2, modt, tw, fg.reshape(1, d), yb)


def _dispatch(idx4, m):
    n_assign = m * TOP_K
    e_flat = idx4.reshape(-1)
    onehot = (e_flat[:, None] == jnp.arange(N_EXPERTS, dtype=jnp.int32)[None, :]).astype(jnp.int32)
    csum = jnp.cumsum(onehot, axis=0)
    rank = jnp.take_along_axis(csum, e_flat[:, None], axis=1)[:, 0] - 1
    counts = csum[-1]
    padded = (counts + MOE_BM - 1) // MOE_BM * MOE_BM
    pad_end = jnp.cumsum(padded)
    pad_start = pad_end - padded
    dest = (pad_start[e_flat] + rank).astype(jnp.int32)
    n_blocks = -(-n_assign // MOE_BM) + N_EXPERTS
    slot_tok = jnp.zeros((n_blocks * MOE_BM,), jnp.int32).at[dest].set(
        jnp.arange(n_assign, dtype=jnp.int32) // TOP_K)
    block_expert = jnp.minimum(
        jnp.searchsorted(pad_end, jnp.arange(n_blocks, dtype=jnp.int32) * MOE_BM, side='right'),
        N_EXPERTS - 1).astype(jnp.int32)
    nvalid = (pad_end[-1:] // MOE_BM).astype(jnp.int32)
    return dest, slot_tok, block_expert, nvalid


def _layer(x2, modt, lp, b, tpb, final_g, last):
    m = x2.shape[0]
    ttot = m // b
    p = _proj_in(x2, lp['norm1_g'], modt, lp['w7'], tpb)

    (r, v, kk, d0, d1, b0, b1, k0, k1, gg, bonus) = _rwkv_prep(
        p, lp['mu'], lp['vec'], lp['w2p'], lp['a2p'], lp['g2'], lp['e'], tpb)
    xt = {'r': r, 'kk': kk, 'd0': d0, 'd1': d1, 'b0': b0, 'b1': b1, 'k0': k0, 'k1': k1}
    yr = _from_scan_y(*_rwkv_scan(xt, _to_scan_v(v, b), b), b)

    na = _na(p, lp['na_bias'], b, tpb)

    u = p[:, CB_U * BRANCH_W:(CB_U + 1) * BRANCH_W]
    u_tb = jnp.transpose(u.reshape(b, ttot, BRANCH_W), (1, 0, 2)).reshape(ttot * b, BRANCH_W)
    to_bt = lambda y: jnp.transpose(y.reshape(ttot, b, BRANCH_W), (1, 0, 2)).reshape(m, BRANCH_W)
    y5f = to_bt(_s5_scan(u_tb, *lp['s5'][0], b, tpb, False))
    y5b = to_bt(_s5_scan(u_tb, *lp['s5'][1], b, tpb, True))

    xn, h2, idx, tw = _merge(x2, modt, lp['norm2_g'], yr, bonus, gg, na, y5f, y5b, p, lp['lnv'], lp['e'],
                             lp['gluw'], lp['wbr'], lp['wout'], lp['rw'], lp['rb'], tpb)

    dest, slot_tok, block_expert, nvalid = _dispatch(idx[:, :TOP_K], m)
    yb = _experts(block_expert, nvalid, slot_tok, h2, lp['guw'], lp['gub'], lp['dnw'], lp['dnb'])
    dest_t = jnp.transpose(dest.reshape(m // TM, TM, TOP_K), (0, 2, 1)).reshape(m // TM, 1, TOP_K * TM)
    return _combine(dest_t, xn, modt, tw, final_g, yb, tpb, last)


def _pad_rows(w, before, total):
    return jnp.pad(w, ((0, 0), (before, total - before - w.shape[1]), (0, 0)))


def kernel(x, c, ctx, c_ctx, ada_w, ada_b, norm1_g, norm2_g, w_in, rwkv_mu_prev, rwkv_mu_next, rwkv_w0, rwkv_w2, rwkv_a0, rwkv_a2, rwkv_g2, rwkv_k_k, rwkv_k_a, rwkv_r_k, rwkv_ln_w, rwkv_ln_b, na_rpb, s5_lambda_re, s5_lambda_im, s5_log_dt, s5_b_re, s5_b_im, s5_c_re, s5_c_im, s5_d, s5_glu_w, s5_glu_b, w_branch, w_out, router_w, router_b, expert_gu_w, expert_gu_b, expert_dn_w, expert_dn_b, final_g):
    b, s, d = x.shape
    l = ctx.shape[1]
    depth = ada_w.shape[0]
    assert d == D_MODEL and l == TM and s % TM == 0 and (s // GRID_W) >= 2 * (TM // GRID_W) and 8 % b == 0
    ttot = l + s
    tpb = ttot // TM
    m = b * ttot

    c8 = jnp.zeros((8, d), f32).at[:b].set(c).at[b].set(c_ctx)
    mods = _ada(c8, ada_w, ada_b).reshape(depth, 8, 6, d)
    eye_h = jnp.kron(jnp.eye(HEADS, dtype=f32), jnp.ones((HEAD, HEAD), f32)).astype(bf16)
    zeros_bw = jnp.zeros((BRANCH_W,), f32)

    x2 = jnp.concatenate([ctx, x], axis=1).reshape(m, d)
    for i in range(depth):
        mi = mods[i]
        modt = jnp.stack([jnp.broadcast_to(mi[b], (b, 6, d)), mi[:b]], axis=1).reshape(2 * b, 6, d)
        modt = jnp.pad(modt, ((0, 0), (0, 2), (0, 0)))
        w = w_in[i]
        w7 = jnp.concatenate([w[:, :RWKV_COLS], jnp.zeros((d, COL_PAD), f32), w[:, RWKV_COLS:]], axis=1).astype(bf16)
        rw = jnp.pad(router_w[i], ((0, 0), (0, LANES - N_EXPERTS)))
        rw_hi = rw.astype(bf16)
        rw_lo = (rw - rw_hi.astype(f32)).astype(bf16)
        lp = {
            'norm1_g': norm1_g[i], 'norm2_g': norm2_g[i], 'w7': w7,
            'mu': jnp.stack([rwkv_mu_prev[i], rwkv_mu_next[i]], 0),
            'vec': jnp.stack([rwkv_k_k[i], rwkv_k_a[i], rwkv_r_k[i].reshape(-1), rwkv_w0[i, 0], rwkv_w0[i, 1],
                              rwkv_a0[i, 0], rwkv_a0[i, 1], zeros_bw], 0),
            'w2p': _pad_rows(rwkv_w2[i], 0, LANES).astype(bf16),
            'a2p': _pad_rows(rwkv_a2[i], LORA_W, LANES).astype(bf16),
            'g2': rwkv_g2[i].astype(bf16), 'e': eye_h,
            'na_bias': _na_bias_tables(na_rpb[i]),
            's5': [_s5_params(s5_lambda_re[i, dd], s5_lambda_im[i, dd], s5_log_dt[i, dd], s5_b_re[i, dd],
                              s5_b_im[i, dd], s5_c_re[i, dd], s5_c_im[i, dd]) for dd in range(2)],
            'lnv': jnp.stack([rwkv_ln_w[i], rwkv_ln_b[i], s5_d[i], s5_glu_b[i]] + [zeros_bw] * 4, 0),
            'gluw': s5_glu_w[i].astype(bf16), 'wbr': w_branch[i].astype(bf16), 'wout': w_out[i].astype(bf16),
            'rw': jnp.stack([rw_hi, rw_lo], 0), 'rb': jnp.pad(router_b[i], (0, LANES - N_EXPERTS)).reshape(1, LANES),
            'guw': expert_gu_w[i].astype(bf16), 'gub': expert_gu_b[i],
            'dnw': expert_dn_w[i].astype(bf16), 'dnb': expert_dn_b[i],
        }
        x2 = _layer(x2, modt, lp, b, tpb, final_g, i == depth - 1)
    return x2.reshape(b, ttot, d)[:, l:]
```

```python
import functools
import math

import numpy as np
import jax
import jax.numpy as jnp
from jax import lax
from jax.experimental import pallas as pl
from jax.experimental.pallas import tpu as pltpu

f32 = jnp.float32
bf16 = jnp.bfloat16

D_MODEL = 1024
BRANCH_W = 512
GRID_W = 64
HEADS = 8
HEAD = 64
LORA_W = 64
LORA_A = 64
LORA_G = 128
RWKV_COLS = 3 * BRANCH_W + LORA_W + LORA_A + LORA_G
RWKV_GN_EPS = 64e-5
NA_KH = 8
NA_KW = 16
S5_GROUP = 16
S5_GROUPS = BRANCH_W // S5_GROUP
S5_STATE = 64
N_EXPERTS = 32
TOP_K = 4
SWIGLU_LIMIT = 7.0
SWIGLU_ALPHA = 1.702
NORM_EPS = 1e-6

TM = 256
LANES = 128
COL_PAD = 256
P_COLS = RWKV_COLS + COL_PAD + 3 * BRANCH_W + BRANCH_W + 3 * D_MODEL
CB_Q, CB_K, CB_V, CB_U = 4, 5, 6, 7
CB_GATE = 4
IN_TN = 1792
NA_NEG = -1e30
S5_CB = 128
S5_NQ = BRANCH_W // S5_CB
S5_BS = (S5_CB // S5_GROUP) * S5_STATE
MOE_BM = 256
VMEM_LIMIT = 56 * 1024 * 1024


def _cparams(sem, **kw):
    return pltpu.CompilerParams(dimension_semantics=sem, vmem_limit_bytes=VMEM_LIMIT, **kw)


def _segsum(x, e):
    hi = x.astype(bf16)
    lo = (x - hi.astype(f32)).astype(bf16)
    return jnp.dot(hi, e, preferred_element_type=f32) + jnp.dot(lo, e, preferred_element_type=f32)


def _sigmoid(x):
    return 1.0 / (1.0 + jnp.exp(-x))


def _softplus(x):
    return jnp.maximum(x, 0.0) + jnp.log(1.0 + jnp.exp(-jnp.abs(x)))


def _rms_mod(x, g, scale, shift):
    y = x * lax.rsqrt(jnp.mean(x * x, axis=-1, keepdims=True) + NORM_EPS)
    return y * g * (1.0 + scale) + shift


def _ada_kernel(c_ref, w_ref, b_ref, o_ref):
    c = c_ref[...]
    act = (c * _sigmoid(c)).astype(bf16)
    o_ref[0] = jnp.dot(act, w_ref[0].astype(bf16), preferred_element_type=f32) + b_ref[0]


def _ada(c8, ada_w, ada_b):
    depth, d, n = ada_w.shape
    tn = 1536
    return pl.pallas_call(
        _ada_kernel,
        grid=(depth, n // tn),
        in_specs=[pl.BlockSpec((8, d), lambda l, j: (0, 0)),
                  pl.BlockSpec((1, d, tn), lambda l, j: (l, 0, j)),
                  pl.BlockSpec((1, 1, tn), lambda l, j: (l, 0, j))],
        out_specs=pl.BlockSpec((1, 8, tn), lambda l, j: (l, 0, j)),
        out_shape=jax.ShapeDtypeStruct((depth, 8, n), f32),
        compiler_params=_cparams(("parallel", "parallel")),
        name="ada_mod",
    )(c8, ada_w, ada_b.reshape(depth, 1, n))


def _in_kernel(x_ref, g_ref, mod_ref, w_ref, o_ref):
    mod = mod_ref[0]
    h = _rms_mod(x_ref[...], g_ref[...], mod[1:2], mod[0:1]).astype(bf16)
    o_ref[...] = jnp.dot(h, w_ref[...], preferred_element_type=f32)


def _seg_of_tile(i, tpb):
    return 2 * (i // tpb) + jnp.minimum(i % tpb, 1)


def _proj_in(x2, g, modt, w7, tpb):
    m, d = x2.shape
    return pl.pallas_call(
        _in_kernel,
        grid=(P_COLS // IN_TN, m // TM),
        in_specs=[pl.BlockSpec((TM, d), lambda j, i: (i, 0)),
                  pl.BlockSpec((1, d), lambda j, i: (0, 0)),
                  pl.BlockSpec((1, 8, d), lambda j, i: (_seg_of_tile(i, tpb), 0, 0)),
                  pl.BlockSpec((d, IN_TN), lambda j, i: (0, j))],
        out_specs=pl.BlockSpec((TM, IN_TN), lambda j, i: (i, j)),
        out_shape=jax.ShapeDtypeStruct((m, P_COLS), f32),
        compiler_params=_cparams(("parallel", "parallel")),
        name="proj_in",
    )(x2, g.reshape(1, d), modt, w7)


def _rwkv_prep_kernel(p_ref, pp_ref, pn_ref, mu_ref, vec_ref, w2_ref, a2_ref, g2_ref, e_ref,
                      r_o, v_o, kk_o, d0_o, d1_o, b0_o, b1_o, k0_o, k1_o, g_o, bonus_o, *, tpb):
    j = pl.program_id(0) % tpb
    prev_ok = jnp.logical_and(j != 0, j != 1)
    next_ok = jnp.logical_and(j != 0, j != tpb - 1)
    p = p_ref[...]
    row = lax.broadcasted_iota(jnp.int32, p.shape, 0)
    hp = jnp.where(prev_ok, pp_ref[7:8, :], 0.0)
    hn = jnp.where(next_ok, pn_ref[0:1, :], 0.0)
    prev = jnp.where(row == 0, hp, pltpu.roll(p, 1, 0))
    nxt = jnp.where(row == TM - 1, hn, pltpu.roll(p, TM - 1, 0))
    mu = mu_ref[...]
    z = p + mu[0:1] * (prev - p) + mu[1:2] * (nxt - p)

    r = z[:, 0:BRANCH_W]
    k = z[:, BRANCH_W:2 * BRANCH_W]
    v = z[:, 2 * BRANCH_W:3 * BRANCH_W]
    wa = z[:, 3 * BRANCH_W:3 * BRANCH_W + LORA_W + LORA_A]
    gd = z[:, 3 * BRANCH_W + LORA_W + LORA_A:RWKV_COLS]
    vec = vec_ref[...]
    e = e_ref[...]
    kk = k * vec[0:1]
    kk = kk / jnp.maximum(jnp.sqrt(_segsum(kk * kk, e)), 1e-12)
    g_o[...] = jnp.dot(_sigmoid(gd).astype(bf16), g2_ref[...], preferred_element_type=f32)
    tw = jnp.tanh(wa).astype(bf16)
    wab = wa.astype(bf16)
    ksum = None
    for d, (dec_o, b_o, k_o) in enumerate(((d0_o, b0_o, k0_o), (d1_o, b1_o, k1_o))):
        wl = vec[3 + d:4 + d] + jnp.dot(tw, w2_ref[d], preferred_element_type=f32)
        w = -_softplus(-wl) - 0.5
        dec_o[...] = jnp.exp(-jnp.exp(w)).T
        a = _sigmoid(vec[5 + d:6 + d] + jnp.dot(wab, a2_ref[d], preferred_element_type=f32))
        kd = k * (1.0 + (a - 1.0) * vec[1:2])
        b_o[...] = (kk * a).T
        k_o[...] = kd.T
        ksum = kd if ksum is None else ksum + kd
    r_o[...] = r.T
    v_o[...] = v
    kk_o[...] = kk.T
    bonus_o[...] = _segsum(r * ksum * vec[2:3], e) * v


def _rwkv_prep(p, mu, vec, w2p, a2p, g2, e, tpb):
    m = p.shape[0]
    bw = BRANCH_W
    full = lambda shape: pl.BlockSpec(shape, lambda i: (0,) * len(shape))
    nat = pl.BlockSpec((TM, bw), lambda i: (i, 0))
    nat_shape = jax.ShapeDtypeStruct((m, bw), f32)
    tr = pl.BlockSpec((None, bw, TM), lambda i: (i // tpb, 0, i % tpb))
    tr_shape = jax.ShapeDtypeStruct((m // (tpb * TM), bw, tpb * TM), f32)
    return pl.pallas_call(
        functools.partial(_rwkv_prep_kernel, tpb=tpb),
        grid=(m // TM,),
        in_specs=[pl.BlockSpec((TM, RWKV_COLS), lambda i: (i, 0)),
                  pl.BlockSpec((8, RWKV_COLS), lambda i: (jnp.maximum(i * (TM // 8) - 1, 0), 0)),
                  pl.BlockSpec((8, RWKV_COLS), lambda i: (jnp.minimum((i + 1) * (TM // 8), m // 8 - 1), 0)),
                  full((2, RWKV_COLS)), full((8, bw)), full((2, LANES, bw)), full((2, LANES, bw)),
                  full((LORA_G, bw)), full((bw, bw))],
        out_specs=[tr, nat, tr] + [tr] * 6 + [nat, nat],
        out_shape=[tr_shape, nat_shape, tr_shape] + [tr_shape] * 6 + [nat_shape, nat_shape],
        compiler_params=_cparams(("parallel",)),
        name="rwkv_prep",
    )(p, p, p, mu, vec, w2p, a2p, g2, e)


SC = 128
KG = HEAD // 4
N_COEF = 5
N_ACC = 1


def _lane_allsum(x, gw):
    x = x + pltpu.roll(x, gw, 1)
    return x + pltpu.roll(x, 2 * gw, 1)


def _rwkv_scan_kernel(wf, bf, kf, kkf, rf, wb, bb, kb, kkb, rb, vf_ref, vb_ref, yf_ref, yb_ref,
                      a4, sf, sb, ypart, *, nb):
    gw = HEADS * nb
    lanes = 4 * gw

    @pl.when(pl.program_id(0) == 0)
    def _():
        sf[...] = jnp.zeros_like(sf)
        sb[...] = jnp.zeros_like(sb)

    srcs = (wf, bf, kf, kkf, rf, wb, bb, kb, kkb, rb)
    for kg in range(KG):
        for vi, src in enumerate(srcs):
            pieces = [src[b, pl.ds(4 * kg + k4, HEADS, stride=HEAD), :] for k4 in range(4) for b in range(nb)]
            a4[vi, kg] = jnp.concatenate(pieces, axis=0).T

    hv = HEAD // 2
    grp = lax.broadcasted_iota(jnp.int32, (hv, lanes), 1) // gw
    zero = jnp.zeros((hv, lanes), f32)

    passes = [(dirn, vh) for dirn in range(2) for vh in range(2)]
    s_refs, v_refs, y_refs = (sf, sb), (vf_ref, vb_ref), (yf_ref, yb_ref)
    time_of = lambda dirn, j: j if dirn == 0 else SC - 1 - j
    vslice = lambda vh: slice(hv * vh, hv * (vh + 1))

    def tree_sum(terms):
        while len(terms) > 1:
            terms = [a + b for a, b in zip(terms[::2], terms[1::2])]
        return terms[0]

    def spread(tq, carry):
        for dirn, vh in passes:
            vs = vslice(vh)
            for g in range(4):
                ypart[dirn, tq * 4 + g, vs, :] = _lane_allsum(jnp.where(grp == g, v_refs[dirn][tq, vs, :], 0.0), gw)
        return carry

    lax.fori_loop(0, SC // 4, spread, 0)

    def first_sa(dirn, vh):
        t0 = time_of(dirn, 0)
        return tree_sum([a4[N_COEF * dirn + 3, kg, t0:t0 + 1, :] * s_refs[dirn][kg, vslice(vh), :]
                         for kg in range(KG)])

    def one_pass(dirn, vh, j, sa):
        s_ref = s_refs[dirn]
        vs = vslice(vh)
        t = time_of(dirn, j)
        tn = jnp.clip(time_of(dirn, j + 1), 0, SC - 1)
        row = lambda vi, kg, tt=t: a4[N_COEF * dirn + vi, kg, pl.ds(tt, 1), :]
        vt = ypart[dirn, t, vs, :]
        ys, ns = [zero] * N_ACC, [zero] * N_ACC
        for kg in range(KG):
            s_new = row(0, kg) * s_ref[kg, vs, :] - row(1, kg) * sa + row(2, kg) * vt
            s_ref[kg, vs, :] = s_new
            ys[kg % N_ACC] = ys[kg % N_ACC] + row(4, kg) * s_new
            ns[kg % N_ACC] = ns[kg % N_ACC] + row(3, kg, tn) * s_new
        ypart[dirn, t, vs, :] = tree_sum(ys)
        return tree_sum(ns)

    def step(j, carry):
        sa_f, part_b = carry[:2], carry[2:]
        sa_b = [_lane_allsum(p, gw) for p in part_b]
        new_f = [_lane_allsum(one_pass(0, vh, j, sa_f[vh]), gw) for vh in range(2)]
        new_b = [one_pass(1, vh, j, sa_b[vh]) for vh in range(2)]
        return tuple(new_f + new_b)

    init = [_lane_allsum(first_sa(0, vh), gw) for vh in range(2)] + [first_sa(1, vh) for vh in range(2)]
    lax.fori_loop(0, SC, step, tuple(init))

    def pack(tq, carry):
        for dirn, vh in passes:
            vs = vslice(vh)
            tile = zero
            for g in range(4):
                tile = jnp.where(grp == g, _lane_allsum(ypart[dirn, tq * 4 + g, vs, :], gw), tile)
            y_refs[dirn][tq, vs, :] = tile
        return carry

    lax.fori_loop(0, SC // 4, pack, 0)


def _rwkv_scan(xt, vt4, nb):
    ttot = xt['r'].shape[2]
    nch = ttot // SC
    nctx = TM // SC
    lanes = 4 * HEADS * nb
    rc = lambda c: jnp.where(c < nctx, nctx - 1 - c, nctx + nch - 1 - c)
    cf = pl.BlockSpec((nb, BRANCH_W, SC), lambda c: (0, 0, c))
    cb = pl.BlockSpec((nb, BRANCH_W, SC), lambda c: (0, 0, rc(c)))
    tf = pl.BlockSpec((SC // 4, HEAD, lanes), lambda c: (c, 0, 0))
    tb = pl.BlockSpec((SC // 4, HEAD, lanes), lambda c: (rc(c), 0, 0))
    out = jax.ShapeDtypeStruct((ttot // 4, HEAD, lanes), f32)
    return pl.pallas_call(
        functools.partial(_rwkv_scan_kernel, nb=nb),
        grid=(nch,),
        in_specs=[cf] * 5 + [cb] * 5 + [tf, tb],
        out_specs=[tf, tb],
        out_shape=[out, out],
        scratch_shapes=[pltpu.VMEM((2 * N_COEF, KG, SC, lanes), f32),
                        pltpu.VMEM((KG, HEAD, lanes), f32), pltpu.VMEM((KG, HEAD, lanes), f32),
                        pltpu.VMEM((2, SC, HEAD, lanes), f32)],
        compiler_params=_cparams(("arbitrary",)),
        name="rwkv_scan",
    )(xt['d0'], xt['b0'], xt['k0'], xt['kk'], xt['r'], xt['d1'], xt['b1'], xt['k1'], xt['kk'], xt['r'], vt4, vt4)


def _to_scan_v(v, b):
    t = v.shape[0] // b
    x = v.reshape(b, t // 4, 4, HEADS, HEAD)
    return jnp.transpose(x, (1, 4, 2, 0, 3)).reshape(t // 4, HEAD, 4 * b * HEADS)


def _from_scan_y(yf4, yb4, b):
    tq = yf4.shape[0]
    x = (yf4 + yb4).reshape(tq, HEAD, 4, b, HEADS)
    return jnp.transpose(x, (3, 0, 2, 4, 1)).reshape(b * tq * 4, BRANCH_W)


def _softmax_pv(q, ks, vs, biases):
    ss = []
    for kt, bias in zip(ks, biases):
        s = lax.dot_general(q, kt, (((1,), (1,)), ((), ())), preferred_element_type=f32)
        ss.append(s if bias is None else s + bias)
    m = ss[0].max(axis=-1, keepdims=True)
    for s in ss[1:]:
        m = jnp.maximum(m, s.max(axis=-1, keepdims=True))
    den = None
    acc = None
    for s, vt in zip(ss, vs):
        pr = jnp.exp(s - m)
        d = pr.sum(axis=-1, keepdims=True)
        o = jnp.dot(pr.astype(bf16), vt, preferred_element_type=f32)
        den = d if den is None else den + d
        acc = o if acc is None else acc + o
    return acc / den


def _na_kernel(q_ref, kc_ref, vc_ref, k0_ref, k1_ref, k2_ref, v0_ref, v1_ref, v2_ref, bm_ref, o_ref):
    j = pl.program_id(1)
    scale = HEAD ** -0.5

    @pl.when(j == 0)
    def _():
        for h in range(HEADS):
            sl = slice(HEAD * h, HEAD * (h + 1))
            q = (q_ref[:, sl] * scale).astype(bf16)
            o_ref[:, sl] = _softmax_pv(q, [kc_ref[:, sl].astype(bf16)], [vc_ref[:, sl].astype(bf16)], [None])

    @pl.when(j > 0)
    def _():
        for h in range(HEADS):
            sl = slice(HEAD * h, HEAD * (h + 1))
            q = (q_ref[:, sl] * scale).astype(bf16)
            ks = [r[:, sl].astype(bf16) for r in (k0_ref, k1_ref, k2_ref, kc_ref)]
            vs = [r[:, sl].astype(bf16) for r in (v0_ref, v1_ref, v2_ref, vc_ref)]
            biases = [bm_ref[0, h, :, TM * s:TM * (s + 1)] for s in range(3)] + [None]
            o_ref[:, sl] = _softmax_pv(q, ks, vs, biases)


def _na(p, biasmask, b, tpb):
    m = p.shape[0]
    nlb = tpb - 1
    bw = BRANCH_W

    def kv(col, slot):
        def imap(bi, j):
            jj = jnp.maximum(j - 1, 0)
            return (bi * tpb + 1 + jnp.clip(jj - 1 + slot, 0, nlb - 1), col)
        return pl.BlockSpec((TM, bw), imap)

    def variant(bi, j):
        jj = jnp.maximum(j - 1, 0)
        return (jnp.where(jj == 0, 0, jnp.where(jj == nlb - 1, 2, 1)), 0, 0, 0)

    return pl.pallas_call(
        _na_kernel,
        grid=(b, tpb),
        in_specs=[pl.BlockSpec((TM, bw), lambda bi, j: (bi * tpb + j, CB_Q)),
                  pl.BlockSpec((TM, bw), lambda bi, j: (bi * tpb, CB_K)),
                  pl.BlockSpec((TM, bw), lambda bi, j: (bi * tpb, CB_V)),
                  kv(CB_K, 0), kv(CB_K, 1), kv(CB_K, 2), kv(CB_V, 0), kv(CB_V, 1), kv(CB_V, 2),
                  pl.BlockSpec((1, HEADS, TM, 3 * TM), variant)],
        out_specs=pl.BlockSpec((TM, bw), lambda bi, j: (bi * tpb + j, 0)),
        out_shape=jax.ShapeDtypeStruct((m, bw), f32),
        compiler_params=_cparams(("parallel", "arbitrary")),
        name="na_attn",
    )(p, p, p, p, p, p, p, p, p, biasmask)


def _na_bias_tables(rpb):
    qr = TM // GRID_W
    dr = np.arange(qr)
    krel = np.arange(3 * qr)
    qc = np.arange(GRID_W)
    kc = np.arange(GRID_W)
    roff = krel[None, :] - dr[:, None] + (NA_KH - 1 - qr)
    coff = kc[None, :] - qc[:, None] + NA_KW - 1
    cs = np.clip(qc - NA_KW // 2, 0, GRID_W - NA_KW)
    colvalid = (kc[None, :] >= cs[:, None]) & (kc[None, :] < cs[:, None] + NA_KW)
    rv_int = (krel[None, :] >= dr[:, None]) & (krel[None, :] < dr[:, None] + NA_KH)
    rv_first = np.broadcast_to((krel >= qr) & (krel < qr + NA_KH), rv_int.shape)
    rv_last = np.broadcast_to(krel < NA_KH, rv_int.shape)
    rowvalid = np.stack([rv_first, rv_int, rv_last], 0)
    valid = rowvalid[:, :, None, :, None] & colvalid[None, None, :, None, :]
    r1 = (roff[:, :, None] == np.arange(2 * NA_KH - 1)).astype(np.float32)
    c1 = (coff[:, :, None] == np.arange(2 * NA_KW - 1)).astype(np.float32)
    tmp = jnp.einsum('rka,hab->hrkb', r1, rpb, precision=lax.Precision.HIGHEST)
    bias = jnp.einsum('hrkb,qcb->hrqkc', tmp, c1, precision=lax.Precision.HIGHEST)
    tab = jnp.where(valid[:, None], bias[None], NA_NEG)
    return tab.reshape(3, HEADS, TM, 3 * TM).astype(f32)


def _s5_kernel(u_ref, wb_ref, wc_ref, a_ref, y_ref, x_s, st_s, *, nb, reverse):
    rows = x_s.shape[0]
    ntile = rows // 8
    nsub = 8 // nb
    ncc = S5_BS // LANES
    shift = (8 - nb) if reverse else nb

    @pl.when(pl.program_id(0) == 0)
    def _():
        st_s[...] = jnp.zeros_like(st_s)

    grp = lax.broadcasted_iota(jnp.int32, (8, LANES), 0) // nb
    for q in range(S5_NQ):
        u = u_ref[:, S5_CB * q:S5_CB * (q + 1)].astype(bf16)
        x_s[...] = jnp.dot(u, wb_ref[q], preferred_element_type=f32)
        ar = [jnp.broadcast_to(a_ref[q, 0:1, LANES * c:LANES * (c + 1)], (8, LANES)) for c in range(ncc)]
        ai = [jnp.broadcast_to(a_ref[q, 1:2, LANES * c:LANES * (c + 1)], (8, LANES)) for c in range(ncc)]

        def tile_step(i, carry):
            rt = (ntile - 1 - i) if reverse else i
            r0 = pl.multiple_of(rt * 8, 8)
            new = []
            for c in range(ncc):
                cr, ci = carry[2 * c], carry[2 * c + 1]
                re_sl = slice(LANES * c, LANES * (c + 1))
                im_sl = slice(S5_BS + LANES * c, S5_BS + LANES * (c + 1))
                br = x_s[pl.ds(r0, 8), re_sl]
                bi = x_s[pl.ds(r0, 8), im_sl]
                out_r = out_i = None
                order = range(nsub - 1, -1, -1) if reverse else range(nsub)
                for s in order:
                    pr = pltpu.roll(cr, shift, 0)
                    pi = pltpu.roll(ci, shift, 0)
                    cr = ar[c] * pr - ai[c] * pi + br
                    ci = ar[c] * pi + ai[c] * pr + bi
                    out_r = cr if out_r is None else jnp.where(grp == s, cr, out_r)
                    out_i = ci if out_i is None else jnp.where(grp == s, ci, out_i)
                x_s[pl.ds(r0, 8), re_sl] = out_r
                x_s[pl.ds(r0, 8), im_sl] = out_i
                new += [out_r, out_i]
            return tuple(new)

        init = tuple(st_s[q, cc] for cc in range(2 * ncc))
        fin = lax.fori_loop(0, ntile, tile_step, init)
        for cc in range(2 * ncc):
            st_s[q, cc] = fin[cc]
        y_ref[:, S5_CB * q:S5_CB * (q + 1)] = jnp.dot(x_s[...].astype(bf16), wc_ref[q],
                                                      preferred_element_type=f32)


def _s5_scan(u_tb, wb, wc, a, nb, tpb, reverse):
    rows = TM * nb
    if reverse:
        imap = lambda c: (jnp.where(c == 0, 0, tpb - c), 0)
    else:
        imap = lambda c: (c, 0)
    full = lambda shape: pl.BlockSpec(shape, lambda c: (0,) * len(shape))
    return pl.pallas_call(
        functools.partial(_s5_kernel, nb=nb, reverse=reverse),
        grid=(tpb,),
        in_specs=[pl.BlockSpec((rows, BRANCH_W), imap),
                  full((S5_NQ, S5_CB, 2 * S5_BS)), full((S5_NQ, 2 * S5_BS, S5_CB)), full((S5_NQ, 2, S5_BS))],
        out_specs=pl.BlockSpec((rows, BRANCH_W), imap),
        out_shape=jax.ShapeDtypeStruct(u_tb.shape, f32),
        scratch_shapes=[pltpu.VMEM((rows, 2 * S5_BS), f32),
                        pltpu.VMEM((S5_NQ, 2 * S5_BS // LANES, 8, LANES), f32)],
        compiler_params=_cparams(("arbitrary",)),
        name="s5_bwd" if reverse else "s5_fwd",
    )(u_tb, wb, wc, a)


def _s5_params(lam_re, lam_im, log_dt, b_re, b_im, c_re, c_im):
    dt = jnp.exp(log_dt)[:, None]
    mag = jnp.exp(lam_re * dt)
    ar = mag * jnp.cos(lam_im * dt)
    ai = mag * jnp.sin(lam_im * dt)
    den = lam_re * lam_re + lam_im * lam_im
    cr = ((ar - 1.0) * lam_re + ai * lam_im) / den
    ci = (ai * lam_re - (ar - 1.0) * lam_im) / den
    bbr = cr[..., None] * b_re - ci[..., None] * b_im
    bbi = cr[..., None] * b_im + ci[..., None] * b_re
    gl = S5_CB // S5_GROUP
    eye = jnp.eye(gl, dtype=f32)

    def wb_half(bb):
        x = bb.reshape(S5_NQ, gl, S5_STATE, S5_GROUP)
        x = jnp.einsum('qgpc,gh->qgchp', x, eye)
        return x.reshape(S5_NQ, S5_CB, S5_BS)

    def wc_half(cc):
        x = cc.reshape(S5_NQ, gl, S5_GROUP, S5_STATE)
        x = jnp.einsum('qgcp,gh->qhpgc', x, eye)
        return x.reshape(S5_NQ, S5_BS, S5_CB)

    wb = jnp.concatenate([wb_half(bbr), wb_half(bbi)], axis=2).astype(bf16)
    wc = jnp.concatenate([wc_half(c_re), wc_half(-c_im)], axis=1).astype(bf16)
    a = jnp.stack([ar.reshape(S5_NQ, S5_BS), ai.reshape(S5_NQ, S5_BS)], axis=1)
    return wb, wc, a


def _gelu_tanh(x):
    return 0.5 * x * (1.0 + jnp.tanh(math.sqrt(2.0 / math.pi) * (x + 0.044715 * x * x * x)))


def _merge_kernel(x_ref, mod_ref, g2n_ref, yr_ref, bonus_ref, gg_ref, na_ref, y5f_ref, y5b_ref, u_ref,
                  ga_ref, gn_ref, gs_ref, lnv_ref, e_ref, gluw_ref, wbr_ref, wout_ref, rw_ref, rb_ref,
                  xo_ref, h2_ref, idx_ref, tw_ref):
    mod = mod_ref[0]
    lnv = lnv_ref[...]
    e = e_ref[...]
    y = yr_ref[...]
    mu = _segsum(y, e) * (1.0 / HEAD)
    dlt = y - mu
    var = _segsum(dlt * dlt, e) * (1.0 / HEAD)
    a_out = (dlt * lax.rsqrt(var + RWKV_GN_EPS) * lnv[0:1] + lnv[1:2] + bonus_ref[...]) * gg_ref[...]
    ys = _gelu_tanh(y5f_ref[...] + y5b_ref[...] + u_ref[...] * lnv[2:3])
    glu = jnp.dot(ys.astype(bf16), gluw_ref[...], preferred_element_type=f32) + lnv[3:4]
    s_out = ys * _sigmoid(glu)
    mix = None
    for o, gate_ref, jdx in ((a_out, ga_ref, 0), (na_ref[...], gn_ref, 1), (s_out, gs_ref, 2)):
        t = _sigmoid(gate_ref[...]) * jnp.dot(o.astype(bf16), wbr_ref[jdx], preferred_element_type=f32)
        mix = t if mix is None else mix + t
    ol = jnp.dot(mix.astype(bf16), wout_ref[...], preferred_element_type=f32)
    xn = x_ref[...] + mod[2:3] * ol
    xo_ref[...] = xn
    h2 = _rms_mod(xn, g2n_ref[...], mod[4:5], mod[3:4])
    h2_ref[...] = h2
    hi = h2.astype(bf16)
    lo = (h2 - hi.astype(f32)).astype(bf16)
    logits = (jnp.dot(hi, rw_ref[0], preferred_element_type=f32) + jnp.dot(lo, rw_ref[0], preferred_element_type=f32)
              + jnp.dot(hi, rw_ref[1], preferred_element_type=f32) + rb_ref[...])
    lane = lax.broadcasted_iota(jnp.int32, logits.shape, 1)
    logits = jnp.where(lane < N_EXPERTS, logits, -jnp.inf)
    idx_acc = jnp.zeros(logits.shape, jnp.int32)
    val_acc = jnp.full(logits.shape, -jnp.inf, f32)
    top = None
    for j in range(TOP_K):
        mx = logits.max(axis=-1, keepdims=True)
        sel = jnp.min(jnp.where(logits == mx, lane, LANES), axis=-1, keepdims=True)
        idx_acc = jnp.where(lane == j, sel, idx_acc)
        val_acc = jnp.where(lane == j, mx, val_acc)
        logits = jnp.where(lane == sel, -jnp.inf, logits)
        top = mx if top is None else top
    ex = jnp.exp(val_acc - top)
    idx_ref[...] = idx_acc
    tw_ref[...] = ex / ex.sum(axis=-1, keepdims=True)


def _merge(x2, modt, g2n, yr, bonus, gg, na, y5f, y5b, p, lnv, e, gluw, wbr, wout, rw, rb, tpb):
    m, d = x2.shape
    bw = BRANCH_W
    tok = lambda w: pl.BlockSpec((TM, w), lambda i: (i, 0))
    full = lambda shape: pl.BlockSpec(shape, lambda i: (0,) * len(shape))
    gate = lambda jdx: pl.BlockSpec((TM, d), lambda i: (i, CB_GATE + jdx))
    return pl.pallas_call(
        _merge_kernel,
        grid=(m // TM,),
        in_specs=[tok(d), pl.BlockSpec((1, 8, d), lambda i: (_seg_of_tile(i, tpb), 0, 0)), full((1, d)),
                  tok(bw), tok(bw), tok(bw), tok(bw), tok(bw), tok(bw),
                  pl.BlockSpec((TM, bw), lambda i: (i, CB_U)), gate(0), gate(1), gate(2),
                  full((8, bw)), full((bw, bw)), full((bw, bw)), full((3, bw, d)), full((d, d)),
                  full((2, d, LANES)), full((1, LANES))],
        out_specs=[tok(d), tok(d), tok(LANES), tok(LANES)],
        out_shape=[jax.ShapeDtypeStruct((m, d), f32), jax.ShapeDtypeStruct((m, d), f32),
                   jax.ShapeDtypeStruct((m, LANES), jnp.int32), jax.ShapeDtypeStruct((m, LANES), f32)],
        compiler_params=_cparams(("parallel",)),
        name="merge_router",
    )(x2, modt, g2n.reshape(1, d), yr, bonus, gg, na, y5f, y5b, p, p, p, p, lnv, e, gluw, wbr, wout, rw, rb)


def _start_row_gather(idx_vmem, idx_smem, src_hbm, dst, sem_i, sem_g, n, unroll=8):
    cp = pltpu.make_async_copy(idx_vmem, idx_smem, sem_i)
    cp.start()
    cp.wait()

    def issue(r, carry):
        pltpu.make_async_copy(src_hbm.at[idx_smem[0, r]], dst.at[r], sem_g).start()
        return carry
    lax.fori_loop(0, n, issue, 0, unroll=unroll)


def _wait_row_gather(src_hbm, dst, sem_g, n):
    pltpu.make_async_copy(src_hbm.at[pl.ds(0, n)], dst, sem_g).wait()


def _expert_kernel(be_ref, nv_ref, tok_ref, tokn_ref, h_hbm, guw_ref, gub_ref, dnw_ref, dnb_ref, y_ref,
                   idx_smem, xb, sem_i, sem_g):
    i = pl.program_id(0)
    nv = nv_ref[0]
    slot = i % 2

    @pl.when(jnp.logical_and(i == 0, nv > 0))
    def _():
        _start_row_gather(tok_ref.at[0], idx_smem, h_hbm, xb.at[0], sem_i, sem_g.at[0], MOE_BM)

    def block(prefetch):
        _wait_row_gather(h_hbm, xb.at[slot], sem_g.at[slot], MOE_BM)
        if prefetch:
            _start_row_gather(tokn_ref.at[0], idx_smem, h_hbm, xb.at[1 - slot], sem_i, sem_g.at[1 - slot],
                              MOE_BM, unroll=True)
        x = xb[slot].astype(bf16)
        gu = jnp.dot(x, guw_ref[0], preferred_element_type=f32) + gub_ref[0]
        glu = jnp.minimum(gu[:, :D_MODEL], SWIGLU_LIMIT)
        lin = jnp.clip(gu[:, D_MODEL:], -SWIGLU_LIMIT, SWIGLU_LIMIT)
        act = glu * _sigmoid(SWIGLU_ALPHA * glu) * (lin + 1.0)
        y_ref[...] = jnp.dot(act.astype(bf16), dnw_ref[0], preferred_element_type=f32) + dnb_ref[0]

    pl.when(i + 1 < nv)(functools.partial(block, True))
    pl.when(i + 1 == nv)(functools.partial(block, False))

    @pl.when(i >= nv_ref[0])
    def _():
        y_ref[...] = jnp.zeros_like(y_ref)


def _experts(block_expert, nvalid, slot_tok, h_rows, guw, gub, dnw, dnb):
    n_blocks = block_expert.shape[0]
    d = D_MODEL
    grid_spec = pltpu.PrefetchScalarGridSpec(
        num_scalar_prefetch=2,
        grid=(n_blocks,),
        in_specs=[pl.BlockSpec((1, 1, MOE_BM), lambda i, be, nv: (i, 0, 0)),
                  pl.BlockSpec((1, 1, MOE_BM), lambda i, be, nv: (jnp.minimum(i + 1, n_blocks - 1), 0, 0)),
                  pl.BlockSpec(memory_space=pl.ANY),
                  pl.BlockSpec((1, d, 2 * d), lambda i, be, nv: (be[i], 0, 0)),
                  pl.BlockSpec((1, 1, 2 * d), lambda i, be, nv: (be[i], 0, 0)),
                  pl.BlockSpec((1, d, d), lambda i, be, nv: (be[i], 0, 0)),
                  pl.BlockSpec((1, 1, d), lambda i, be, nv: (be[i], 0, 0))],
        out_specs=pl.BlockSpec((MOE_BM, d), lambda i, be, nv: (i, 0)),
        scratch_shapes=[pltpu.SMEM((1, MOE_BM), jnp.int32), pltpu.VMEM((2, MOE_BM, d), f32),
                        pltpu.SemaphoreType.DMA, pltpu.SemaphoreType.DMA((2,))],
    )
    tok3 = slot_tok.reshape(n_blocks, 1, MOE_BM)
    return pl.pallas_call(
        _expert_kernel,
        grid_spec=grid_spec,
        out_shape=jax.ShapeDtypeStruct((n_blocks * MOE_BM, d), f32),
        compiler_params=_cparams(("arbitrary",), disable_bounds_checks=True),
        name="moe_experts",
    )(block_expert, nvalid, tok3, tok3, h_rows, guw,
      gub.reshape(N_EXPERTS, 1, 2 * d), dnw, dnb.reshape(N_EXPERTS, 1, d))


def _combine_kernel(dest_ref, x_ref, mod_ref, tw_ref, fg_ref, yb_hbm, o_ref, idx_smem, buf, sem_i, sem_g, *, final):
    _start_row_gather(dest_ref.at[0], idx_smem, yb_hbm, buf, sem_i, sem_g, TOP_K * TM)
    _wait_row_gather(yb_hbm, buf, sem_g, TOP_K * TM)
    tw = tw_ref[...]
    y = None
    for j in range(TOP_K):
        t = tw[:, j:j + 1] * buf[TM * j:TM * (j + 1), :]
        y = t if y is None else y + t
    xn = x_ref[...] + mod_ref[0][5:6] * y
    if final:
        xn = xn * lax.rsqrt(jnp.mean(xn * xn, axis=-1, keepdims=True) + NORM_EPS) * fg_ref[...]
    o_ref[...] = xn


def _combine(dest_t, x2, modt, tw, fg, yb, tpb, final):
    m, d = x2.shape
    return pl.pallas_call(
        functools.partial(_combine_kernel, final=final),
        grid=(m // TM,),
        in_specs=[pl.BlockSpec((1, 1, TOP_K * TM), lambda i: (i, 0, 0)),
                  pl.BlockSpec((TM, d), lambda i: (i, 0)),
                  pl.BlockSpec((1, 8, d), lambda i: (_seg_of_tile(i, tpb), 0, 0)),
                  pl.BlockSpec((TM, LANES), lambda i: (i, 0)),
                  pl.BlockSpec((1, d), lambda i: (0, 0)),
                  pl.BlockSpec(memory_space=pl.ANY)],
        out_specs=pl.BlockSpec((TM, d), lambda i: (i, 0)),
        out_shape=jax.ShapeDtypeStruct((m, d), f32),
        scratch_shapes=[pltpu.SMEM((1, TOP_K * TM), jnp.int32), pltpu.VMEM((TOP_K * TM, d), f32),
                        pltpu.SemaphoreType.DMA, pltpu.SemaphoreType.DMA],
        compiler_params=_cparams(("arbitrary",), disable_bounds_checks=True),
        name="moe_combine",
    )(dest_t, x2, modt, tw, fg.reshape(1, d), yb)


def _dispatch(idx4, m):
    n_assign = m * TOP_K
    e_flat = idx4.reshape(-1)
    onehot = (e_flat[:, None] == jnp.arange(N_EXPERTS, dtype=jnp.int32)[None, :]).astype(jnp.int32)
    csum = jnp.cumsum(onehot, axis=0)
    rank = jnp.take_along_axis(csum, e_flat[:, None], axis=1)[:, 0] - 1
    counts = csum[-1]
    padded = (counts + MOE_BM - 1) // MOE_BM * MOE_BM
    pad_end = jnp.cumsum(padded)
    pad_start = pad_end - padded
    dest = (pad_start[e_flat] + rank).astype(jnp.int32)
    n_blocks = -(-n_assign // MOE_BM) + N_EXPERTS
    slot_tok = jnp.zeros((n_blocks * MOE_BM,), jnp.int32).at[dest].set(
        jnp.arange(n_assign, dtype=jnp.int32) // TOP_K)
    block_expert = jnp.minimum(
        jnp.searchsorted(pad_end, jnp.arange(n_blocks, dtype=jnp.int32) * MOE_BM, side='right'),
        N_EXPERTS - 1).astype(jnp.int32)
    nvalid = (pad_end[-1:] // MOE_BM).astype(jnp.int32)
    return dest, slot_tok, block_expert, nvalid


def _layer(x2, modt, lp, b, tpb, final_g, last):
    m = x2.shape[0]
    ttot = m // b
    p = _proj_in(x2, lp['norm1_g'], modt, lp['w7'], tpb)

    (r, v, kk, d0, d1, b0, b1, k0, k1, gg, bonus) = _rwkv_prep(
        p, lp['mu'], lp['vec'], lp['w2p'], lp['a2p'], lp['g2'], lp['e'], tpb)
    xt = {'r': r, 'kk': kk, 'd0': d0, 'd1': d1, 'b0': b0, 'b1': b1, 'k0': k0, 'k1': k1}
    yr = _from_scan_y(*_rwkv_scan(xt, _to_scan_v(v, b), b), b)

    na = _na(p, lp['na_bias'], b, tpb)

    u = p[:, CB_U * BRANCH_W:(CB_U + 1) * BRANCH_W]
    u_tb = jnp.transpose(u.reshape(b, ttot, BRANCH_W), (1, 0, 2)).reshape(ttot * b, BRANCH_W)
    to_bt = lambda y: jnp.transpose(y.reshape(ttot, b, BRANCH_W), (1, 0, 2)).reshape(m, BRANCH_W)
    y5f = to_bt(_s5_scan(u_tb, *lp['s5'][0], b, tpb, False))
    y5b = to_bt(_s5_scan(u_tb, *lp['s5'][1], b, tpb, True))

    xn, h2, idx, tw = _merge(x2, modt, lp['norm2_g'], yr, bonus, gg, na, y5f, y5b, p, lp['lnv'], lp['e'],
                             lp['gluw'], lp['wbr'], lp['wout'], lp['rw'], lp['rb'], tpb)

    dest, slot_tok, block_expert, nvalid = _dispatch(idx[:, :TOP_K], m)
    yb = _experts(block_expert, nvalid, slot_tok, h2, lp['guw'], lp['gub'], lp['dnw'], lp['dnb'])
    dest_t = jnp.transpose(dest.reshape(m // TM, TM, TOP_K), (0, 2, 1)).reshape(m // TM, 1, TOP_K * TM)
    return _combine(dest_t, xn, modt, tw, final_g, yb, tpb, last)


def _pad_rows(w, before, total):
    return jnp.pad(w, ((0, 0), (before, total - before - w.shape[1]), (0, 0)))


def kernel(x, c, ctx, c_ctx, ada_w, ada_b, norm1_g, norm2_g, w_in, rwkv_mu_prev, rwkv_mu_next, rwkv_w0, rwkv_w2, rwkv_a0, rwkv_a2, rwkv_g2, rwkv_k_k, rwkv_k_a, rwkv_r_k, rwkv_ln_w, rwkv_ln_b, na_rpb, s5_lambda_re, s5_lambda_im, s5_log_dt, s5_b_re, s5_b_im, s5_c_re, s5_c_im, s5_d, s5_glu_w, s5_glu_b, w_branch, w_out, router_w, router_b, expert_gu_w, expert_gu_b, expert_dn_w, expert_dn_b, final_g):
    b, s, d = x.shape
    l = ctx.shape[1]
    depth = ada_w.shape[0]
    assert d == D_MODEL and l == TM and s % TM == 0 and (s // GRID_W) >= 2 * (TM // GRID_W) and 8 % b == 0
    ttot = l + s
    tpb = ttot // TM
    m = b * ttot

    c8 = jnp.zeros((8, d), f32).at[:b].set(c).at[b].set(c_ctx)
    mods = _ada(c8, ada_w, ada_b).reshape(depth, 8, 6, d)
    eye_h = jnp.kron(jnp.eye(HEADS, dtype=f32), jnp.ones((HEAD, HEAD), f32)).astype(bf16)
    zeros_bw = jnp.zeros((BRANCH_W,), f32)

    x2 = jnp.concatenate([ctx, x], axis=1).reshape(m, d)
    for i in range(depth):
        mi = mods[i]
        modt = jnp.stack([jnp.broadcast_to(mi[b], (b, 6, d)), mi[:b]], axis=1).reshape(2 * b, 6, d)
        modt = jnp.pad(modt, ((0, 0), (0, 2), (0, 0)))
        w = w_in[i]
        w7 = jnp.concatenate([w[:, :RWKV_COLS], jnp.zeros((d, COL_PAD), f32), w[:, RWKV_COLS:]], axis=1).astype(bf16)
        rw = jnp.pad(router_w[i], ((0, 0), (0, LANES - N_EXPERTS)))
        rw_hi = rw.astype(bf16)
        rw_lo = (rw - rw_hi.astype(f32)).astype(bf16)
        lp = {
            'norm1_g': norm1_g[i], 'norm2_g': norm2_g[i], 'w7': w7,
            'mu': jnp.stack([rwkv_mu_prev[i], rwkv_mu_next[i]], 0),
            'vec': jnp.stack([rwkv_k_k[i], rwkv_k_a[i], rwkv_r_k[i].reshape(-1), rwkv_w0[i, 0], rwkv_w0[i, 1],
                              rwkv_a0[i, 0], rwkv_a0[i, 1], zeros_bw], 0),
            'w2p': _pad_rows(rwkv_w2[i], 0, LANES).astype(bf16),
            'a2p': _pad_rows(rwkv_a2[i], LORA_W, LANES).astype(bf16),
            'g2': rwkv_g2[i].astype(bf16), 'e': eye_h,
            'na_bias': _na_bias_tables(na_rpb[i]),
            's5': [_s5_params(s5_lambda_re[i, dd], s5_lambda_im[i, dd], s5_log_dt[i, dd], s5_b_re[i, dd],
                              s5_b_im[i, dd], s5_c_re[i, dd], s5_c_im[i, dd]) for dd in range(2)],
            'lnv': jnp.stack([rwkv_ln_w[i], rwkv_ln_b[i], s5_d[i], s5_glu_b[i]] + [zeros_bw] * 4, 0),
            'gluw': s5_glu_w[i].astype(bf16), 'wbr': w_branch[i].astype(bf16), 'wout': w_out[i].astype(bf16),
            'rw': jnp.stack([rw_hi, rw_lo], 0), 'rb': jnp.pad(router_b[i], (0, LANES - N_EXPERTS)).reshape(1, LANES),
            'guw': expert_gu_w[i].astype(bf16), 'gub': expert_gu_b[i],
            'dnw': expert_dn_w[i].astype(bf16), 'dnb': expert_dn_b[i],
        }
        x2 = _layer(x2, modt, lp, b, tpb, final_g, i == depth - 1)
    return x2.reshape(b, ttot, d)[:, l:]
```

```python
import functools
import math

import numpy as np
import jax
import jax.numpy as jnp
from jax import lax
from jax.experimental import pallas as pl
from jax.experimental.pallas import tpu as pltpu

f32 = jnp.float32
bf16 = jnp.bfloat16

D_MODEL = 1024
BRANCH_W = 512
GRID_W = 64
HEADS = 8
HEAD = 64
LORA_W = 64
LORA_A = 64
LORA_G = 128
RWKV_COLS = 3 * BRANCH_W + LORA_W + LORA_A + LORA_G
RWKV_GN_EPS = 64e-5
NA_KH = 8
NA_KW = 16
S5_GROUP = 16
S5_GROUPS = BRANCH_W // S5_GROUP
S5_STATE = 64
N_EXPERTS = 32
TOP_K = 4
SWIGLU_LIMIT = 7.0
SWIGLU_ALPHA = 1.702
NORM_EPS = 1e-6

TM = 256
LANES = 128
COL_PAD = 256
P_COLS = RWKV_COLS + COL_PAD + 3 * BRANCH_W + BRANCH_W + 3 * D_MODEL
CB_Q, CB_K, CB_V, CB_U = 4, 5, 6, 7
CB_GATE = 4
IN_TN = 1792
NA_NEG = -1e30
S5_CB = 128
S5_NQ = BRANCH_W // S5_CB
S5_BS = (S5_CB // S5_GROUP) * S5_STATE
MOE_BM = 256
VMEM_LIMIT = 56 * 1024 * 1024


def _cparams(sem, **kw):
    return pltpu.CompilerParams(dimension_semantics=sem, vmem_limit_bytes=VMEM_LIMIT, **kw)


def _segsum(x, e):
    hi = x.astype(bf16)
    lo = (x - hi.astype(f32)).astype(bf16)
    return jnp.dot(hi, e, preferred_element_type=f32) + jnp.dot(lo, e, preferred_element_type=f32)


def _sigmoid(x):
    return 1.0 / (1.0 + jnp.exp(-x))


def _softplus(x):
    return jnp.maximum(x, 0.0) + jnp.log(1.0 + jnp.exp(-jnp.abs(x)))


def _rms_mod(x, g, scale, shift):
    y = x * lax.rsqrt(jnp.mean(x * x, axis=-1, keepdims=True) + NORM_EPS)
    return y * g * (1.0 + scale) + shift


def _ada_kernel(c_ref, w_ref, b_ref, o_ref):
    c = c_ref[...]
    act = (c * _sigmoid(c)).astype(bf16)
    o_ref[0] = jnp.dot(act, w_ref[0].astype(bf16), preferred_element_type=f32) + b_ref[0]


def _ada(c8, ada_w, ada_b):
    depth, d, n = ada_w.shape
    tn = 1536
    return pl.pallas_call(
        _ada_kernel,
        grid=(depth, n // tn),
        in_specs=[pl.BlockSpec((8, d), lambda l, j: (0, 0)),
                  pl.BlockSpec((1, d, tn), lambda l, j: (l, 0, j)),
                  pl.BlockSpec((1, 1, tn), lambda l, j: (l, 0, j))],
        out_specs=pl.BlockSpec((1, 8, tn), lambda l, j: (l, 0, j)),
        out_shape=jax.ShapeDtypeStruct((depth, 8, n), f32),
        compiler_params=_cparams(("parallel", "parallel")),
        name="ada_mod",
    )(c8, ada_w, ada_b.reshape(depth, 1, n))


def _in_kernel(x_ref, g_ref, mod_ref, w_ref, o_ref):
    mod = mod_ref[0]
    h = _rms_mod(x_ref[...], g_ref[...], mod[1:2], mod[0:1]).astype(bf16)
    o_ref[...] = jnp.dot(h, w_ref[...], preferred_element_type=f32)


def _seg_of_tile(i, tpb):
    return 2 * (i // tpb) + jnp.minimum(i % tpb, 1)


def _proj_in(x2, g, modt, w7, tpb):
    m, d = x2.shape
    return pl.pallas_call(
        _in_kernel,
        grid=(P_COLS // IN_TN, m // TM),
        in_specs=[pl.BlockSpec((TM, d), lambda j, i: (i, 0)),
                  pl.BlockSpec((1, d), lambda j, i: (0, 0)),
                  pl.BlockSpec((1, 8, d), lambda j, i: (_seg_of_tile(i, tpb), 0, 0)),
                  pl.BlockSpec((d, IN_TN), lambda j, i: (0, j))],
        out_specs=pl.BlockSpec((TM, IN_TN), lambda j, i: (i, j)),
        out_shape=jax.ShapeDtypeStruct((m, P_COLS), f32),
        compiler_params=_cparams(("parallel", "parallel")),
        name="proj_in",
    )(x2, g.reshape(1, d), modt, w7)


def _rwkv_prep_kernel(p_ref, pp_ref, pn_ref, mu_ref, vec_ref, w2_ref, a2_ref, g2_ref, e_ref,
                      r_o, v_o, kk_o, d0_o, d1_o, b0_o, b1_o, k0_o, k1_o, g_o, bonus_o, *, tpb):
    j = pl.program_id(0) % tpb
    prev_ok = jnp.logical_and(j != 0, j != 1)
    next_ok = jnp.logical_and(j != 0, j != tpb - 1)
    p = p_ref[...]
    row = lax.broadcasted_iota(jnp.int32, p.shape, 0)
    hp = jnp.where(prev_ok, pp_ref[7:8, :], 0.0)
    hn = jnp.where(next_ok, pn_ref[0:1, :], 0.0)
    prev = jnp.where(row == 0, hp, pltpu.roll(p, 1, 0))
    nxt = jnp.where(row == TM - 1, hn, pltpu.roll(p, TM - 1, 0))
    mu = mu_ref[...]
    z = p + mu[0:1] * (prev - p) + mu[1:2] * (nxt - p)

    r = z[:, 0:BRANCH_W]
    k = z[:, BRANCH_W:2 * BRANCH_W]
    v = z[:, 2 * BRANCH_W:3 * BRANCH_W]
    wa = z[:, 3 * BRANCH_W:3 * BRANCH_W + LORA_W + LORA_A]
    gd = z[:, 3 * BRANCH_W + LORA_W + LORA_A:RWKV_COLS]
    vec = vec_ref[...]
    e = e_ref[...]
    kk = k * vec[0:1]
    kk = kk / jnp.maximum(jnp.sqrt(_segsum(kk * kk, e)), 1e-12)
    g_o[...] = jnp.dot(_sigmoid(gd).astype(bf16), g2_ref[...], preferred_element_type=f32)
    tw = jnp.tanh(wa).astype(bf16)
    wab = wa.astype(bf16)
    ksum = None
    for d, (dec_o, b_o, k_o) in enumerate(((d0_o, b0_o, k0_o), (d1_o, b1_o, k1_o))):
        wl = vec[3 + d:4 + d] + jnp.dot(tw, w2_ref[d], preferred_element_type=f32)
        w = -_softplus(-wl) - 0.5
        dec_o[...] = jnp.exp(-jnp.exp(w)).T
        a = _sigmoid(vec[5 + d:6 + d] + jnp.dot(wab, a2_ref[d], preferred_element_type=f32))
        kd = k * (1.0 + (a - 1.0) * vec[1:2])
        b_o[...] = (kk * a).T
        k_o[...] = kd.T
        ksum = kd if ksum is None else ksum + kd
    r_o[...] = r.T
    v_o[...] = v
    kk_o[...] = kk.T
    bonus_o[...] = _segsum(r * ksum * vec[2:3], e) * v


def _rwkv_prep(p, mu, vec, w2p, a2p, g2, e, tpb):
    m = p.shape[0]
    bw = BRANCH_W
    full = lambda shape: pl.BlockSpec(shape, lambda i: (0,) * len(shape))
    nat = pl.BlockSpec((TM, bw), lambda i: (i, 0))
    nat_shape = jax.ShapeDtypeStruct((m, bw), f32)
    tr = pl.BlockSpec((None, bw, TM), lambda i: (i // tpb, 0, i % tpb))
    tr_shape = jax.ShapeDtypeStruct((m // (tpb * TM), bw, tpb * TM), f32)
    return pl.pallas_call(
        functools.partial(_rwkv_prep_kernel, tpb=tpb),
        grid=(m // TM,),
        in_specs=[pl.BlockSpec((TM, RWKV_COLS), lambda i: (i, 0)),
                  pl.BlockSpec((8, RWKV_COLS), lambda i: (jnp.maximum(i * (TM // 8) - 1, 0), 0)),
                  pl.BlockSpec((8, RWKV_COLS), lambda i: (jnp.minimum((i + 1) * (TM // 8), m // 8 - 1), 0)),
                  full((2, RWKV_COLS)), full((8, bw)), full((2, LANES, bw)), full((2, LANES, bw)),
                  full((LORA_G, bw)), full((bw, bw))],
        out_specs=[tr, nat, tr] + [tr] * 6 + [nat, nat],
        out_shape=[tr_shape, nat_shape, tr_shape] + [tr_shape] * 6 + [nat_shape, nat_shape],
        compiler_params=_cparams(("parallel",)),
        name="rwkv_prep",
    )(p, p, p, mu, vec, w2p, a2p, g2, e)


SC = 128
KG = HEAD // 4
N_COEF = 5
N_ACC = 1


def _lane_allsum(x, gw):
    x = x + pltpu.roll(x, gw, 1)
    return x + pltpu.roll(x, 2 * gw, 1)


def _rwkv_scan_kernel(wf, bf, kf, kkf, rf, wb, bb, kb, kkb, rb, vf_ref, vb_ref, yf_ref, yb_ref,
                      a4, sf, sb, ypart, *, nb):
    gw = HEADS * nb
    lanes = 4 * gw

    @pl.when(pl.program_id(0) == 0)
    def _():
        sf[...] = jnp.zeros_like(sf)
        sb[...] = jnp.zeros_like(sb)

    srcs = (wf, bf, kf, kkf, rf, wb, bb, kb, kkb, rb)
    for kg in range(KG):
        for vi, src in enumerate(srcs):
            pieces = [src[b, pl.ds(4 * kg + k4, HEADS, stride=HEAD), :] for k4 in range(4) for b in range(nb)]
            a4[vi, kg] = jnp.concatenate(pieces, axis=0).T

    hv = HEAD // 2
    grp = lax.broadcasted_iota(jnp.int32, (hv, lanes), 1) // gw
    zero = jnp.zeros((hv, lanes), f32)

    passes = [(dirn, vh) for dirn in range(2) for vh in range(2)]
    s_refs, v_refs, y_refs = (sf, sb), (vf_ref, vb_ref), (yf_ref, yb_ref)
    time_of = lambda dirn, j: j if dirn == 0 else SC - 1 - j
    vslice = lambda vh: slice(hv * vh, hv * (vh + 1))

    def tree_sum(terms):
        while len(terms) > 1:
            terms = [a + b for a, b in zip(terms[::2], terms[1::2])]
        return terms[0]

    def spread(tq, carry):
        for dirn, vh in passes:
            vs = vslice(vh)
            x = v_refs[dirn][tq, vs, :]
            xs = [x] + [pltpu.roll(x, s * gw, 1) for s in range(1, 4)]
            for g in range(4):
                pick = lambda j: xs[(j - g) % 4]
                ypart[dirn, tq * 4 + g, vs, :] = jnp.where(
                    grp == 0, pick(0), jnp.where(grp == 1, pick(1), jnp.where(grp == 2, pick(2), pick(3))))
        return carry

    lax.fori_loop(0, SC // 4, spread, 0)

    def first_sa(dirn, vh):
        t0 = time_of(dirn, 0)
        return tree_sum([a4[N_COEF * dirn + 3, kg, t0:t0 + 1, :] * s_refs[dirn][kg, vslice(vh), :]
                         for kg in range(KG)])

    def one_pass(dirn, vh, j, sa):
        s_ref = s_refs[dirn]
        vs = vslice(vh)
        t = time_of(dirn, j)
        tn = jnp.clip(time_of(dirn, j + 1), 0, SC - 1)
        row = lambda vi, kg, tt=t: a4[N_COEF * dirn + vi, kg, pl.ds(tt, 1), :]
        vt = ypart[dirn, t, vs, :]
        ys, ns = [zero] * N_ACC, [zero] * N_ACC
        for kg in range(KG):
            s_new = row(0, kg) * s_ref[kg, vs, :] - row(1, kg) * sa + row(2, kg) * vt
            s_ref[kg, vs, :] = s_new
            ys[kg % N_ACC] = ys[kg % N_ACC] + row(4, kg) * s_new
            ns[kg % N_ACC] = ns[kg % N_ACC] + row(3, kg, tn) * s_new
        ypart[dirn, t, vs, :] = tree_sum(ys)
        return tree_sum(ns)

    def step(j, carry):
        sa_f, part_b = carry[:2], carry[2:]
        sa_b = [_lane_allsum(p, gw) for p in part_b]
        new_f = [_lane_allsum(one_pass(0, vh, j, sa_f[vh]), gw) for vh in range(2)]
        new_b = [one_pass(1, vh, j, sa_b[vh]) for vh in range(2)]
        return tuple(new_f + new_b)

    init = [_lane_allsum(first_sa(0, vh), gw) for vh in range(2)] + [first_sa(1, vh) for vh in range(2)]
    lax.fori_loop(0, SC, step, tuple(init))

    def pack(tq, carry):
        for dirn, vh in passes:
            vs = vslice(vh)
            z = [ypart[dirn, tq * 4 + g, vs, :] for g in range(4)]
            tile = None
            for s in range(4):
                u = jnp.where(grp == (0 - s) % 4, z[0],
                              jnp.where(grp == (1 - s) % 4, z[1], jnp.where(grp == (2 - s) % 4, z[2], z[3])))
                u = u if s == 0 else pltpu.roll(u, s * gw, 1)
                tile = u if tile is None else tile + u
            y_refs[dirn][tq, vs, :] = tile
        return carry

    lax.fori_loop(0, SC // 4, pack, 0)


def _rwkv_scan(xt, vt4, nb):
    ttot = xt['r'].shape[2]
    nch = ttot // SC
    nctx = TM // SC
    lanes = 4 * HEADS * nb
    rc = lambda c: jnp.where(c < nctx, nctx - 1 - c, nctx + nch - 1 - c)
    cf = pl.BlockSpec((nb, BRANCH_W, SC), lambda c: (0, 0, c))
    cb = pl.BlockSpec((nb, BRANCH_W, SC), lambda c: (0, 0, rc(c)))
    tf = pl.BlockSpec((SC // 4, HEAD, lanes), lambda c: (c, 0, 0))
    tb = pl.BlockSpec((SC // 4, HEAD, lanes), lambda c: (rc(c), 0, 0))
    out = jax.ShapeDtypeStruct((ttot // 4, HEAD, lanes), f32)
    return pl.pallas_call(
        functools.partial(_rwkv_scan_kernel, nb=nb),
        grid=(nch,),
        in_specs=[cf] * 5 + [cb] * 5 + [tf, tb],
        out_specs=[tf, tb],
        out_shape=[out, out],
        scratch_shapes=[pltpu.VMEM((2 * N_COEF, KG, SC, lanes), f32),
                        pltpu.VMEM((KG, HEAD, lanes), f32), pltpu.VMEM((KG, HEAD, lanes), f32),
                        pltpu.VMEM((2, SC, HEAD, lanes), f32)],
        compiler_params=_cparams(("arbitrary",)),
        name="rwkv_scan",
    )(xt['d0'], xt['b0'], xt['k0'], xt['kk'], xt['r'], xt['d1'], xt['b1'], xt['k1'], xt['kk'], xt['r'], vt4, vt4)


def _to_scan_v(v, b):
    t = v.shape[0] // b
    x = v.reshape(b, t // 4, 4, HEADS, HEAD)
    return jnp.transpose(x, (1, 4, 2, 0, 3)).reshape(t // 4, HEAD, 4 * b * HEADS)


def _from_scan_y(yf4, yb4, b):
    tq = yf4.shape[0]
    x = (yf4 + yb4).reshape(tq, HEAD, 4, b, HEADS)
    return jnp.transpose(x, (3, 0, 2, 4, 1)).reshape(b * tq * 4, BRANCH_W)


def _softmax_pv(q, ks, vs, biases):
    ss = []
    for kt, bias in zip(ks, biases):
        s = lax.dot_general(q, kt, (((1,), (1,)), ((), ())), preferred_element_type=f32)
        ss.append(s if bias is None else s + bias)
    m = ss[0].max(axis=-1, keepdims=True)
    for s in ss[1:]:
        m = jnp.maximum(m, s.max(axis=-1, keepdims=True))
    den = None
    acc = None
    for s, vt in zip(ss, vs):
        pr = jnp.exp(s - m)
        d = pr.sum(axis=-1, keepdims=True)
        o = jnp.dot(pr.astype(bf16), vt, preferred_element_type=f32)
        den = d if den is None else den + d
        acc = o if acc is None else acc + o
    return acc / den


def _na_kernel(q_ref, kc_ref, vc_ref, k0_ref, k1_ref, k2_ref, v0_ref, v1_ref, v2_ref, bm_ref, o_ref):
    j = pl.program_id(1)
    scale = HEAD ** -0.5

    @pl.when(j == 0)
    def _():
        for h in range(HEADS):
            sl = slice(HEAD * h, HEAD * (h + 1))
            q = (q_ref[:, sl] * scale).astype(bf16)
            o_ref[:, sl] = _softmax_pv(q, [kc_ref[:, sl].astype(bf16)], [vc_ref[:, sl].astype(bf16)], [None])

    @pl.when(j > 0)
    def _():
        for h in range(HEADS):
            sl = slice(HEAD * h, HEAD * (h + 1))
            q = (q_ref[:, sl] * scale).astype(bf16)
            ks = [r[:, sl].astype(bf16) for r in (k0_ref, k1_ref, k2_ref, kc_ref)]
            vs = [r[:, sl].astype(bf16) for r in (v0_ref, v1_ref, v2_ref, vc_ref)]
            biases = [bm_ref[0, h, :, TM * s:TM * (s + 1)] for s in range(3)] + [None]
            o_ref[:, sl] = _softmax_pv(q, ks, vs, biases)


def _na(p, biasmask, b, tpb):
    m = p.shape[0]
    nlb = tpb - 1
    bw = BRANCH_W

    def kv(col, slot):
        def imap(bi, j):
            jj = jnp.maximum(j - 1, 0)
            return (bi * tpb + 1 + jnp.clip(jj - 1 + slot, 0, nlb - 1), col)
        return pl.BlockSpec((TM, bw), imap)

    def variant(bi, j):
        jj = jnp.maximum(j - 1, 0)
        return (jnp.where(jj == 0, 0, jnp.where(jj == nlb - 1, 2, 1)), 0, 0, 0)

    return pl.pallas_call(
        _na_kernel,
        grid=(b, tpb),
        in_specs=[pl.BlockSpec((TM, bw), lambda bi, j: (bi * tpb + j, CB_Q)),
                  pl.BlockSpec((TM, bw), lambda bi, j: (bi * tpb, CB_K)),
                  pl.BlockSpec((TM, bw), lambda bi, j: (bi * tpb, CB_V)),
                  kv(CB_K, 0), kv(CB_K, 1), kv(CB_K, 2), kv(CB_V, 0), kv(CB_V, 1), kv(CB_V, 2),
                  pl.BlockSpec((1, HEADS, TM, 3 * TM), variant)],
        out_specs=pl.BlockSpec((TM, bw), lambda bi, j: (bi * tpb + j, 0)),
        out_shape=jax.ShapeDtypeStruct((m, bw), f32),
        compiler_params=_cparams(("parallel", "arbitrary")),
        name="na_attn",
    )(p, p, p, p, p, p, p, p, p, biasmask)


def _na_bias_tables(rpb):
    qr = TM // GRID_W
    dr = np.arange(qr)
    krel = np.arange(3 * qr)
    qc = np.arange(GRID_W)
    kc = np.arange(GRID_W)
    roff = krel[None, :] - dr[:, None] + (NA_KH - 1 - qr)
    coff = kc[None, :] - qc[:, None] + NA_KW - 1
    cs = np.clip(qc - NA_KW // 2, 0, GRID_W - NA_KW)
    colvalid = (kc[None, :] >= cs[:, None]) & (kc[None, :] < cs[:, None] + NA_KW)
    rv_int = (krel[None, :] >= dr[:, None]) & (krel[None, :] < dr[:, None] + NA_KH)
    rv_first = np.broadcast_to((krel >= qr) & (krel < qr + NA_KH), rv_int.shape)
    rv_last = np.broadcast_to(krel < NA_KH, rv_int.shape)
    rowvalid = np.stack([rv_first, rv_int, rv_last], 0)
    valid = rowvalid[:, :, None, :, None] & colvalid[None, None, :, None, :]
    r1 = (roff[:, :, None] == np.arange(2 * NA_KH - 1)).astype(np.float32)
    c1 = (coff[:, :, None] == np.arange(2 * NA_KW - 1)).astype(np.float32)
    tmp = jnp.einsum('rka,hab->hrkb', r1, rpb, precision=lax.Precision.HIGHEST)
    bias = jnp.einsum('hrkb,qcb->hrqkc', tmp, c1, precision=lax.Precision.HIGHEST)
    tab = jnp.where(valid[:, None], bias[None], NA_NEG)
    return tab.reshape(3, HEADS, TM, 3 * TM).astype(f32)


def _s5_kernel(u_ref, wb_ref, wc_ref, a_ref, y_ref, x_s, st_s, *, nb, reverse):
    rows = x_s.shape[0]
    ntile = rows // 8
    nsub = 8 // nb
    ncc = S5_BS // LANES
    shift = (8 - nb) if reverse else nb

    @pl.when(pl.program_id(0) == 0)
    def _():
        st_s[...] = jnp.zeros_like(st_s)

    grp = lax.broadcasted_iota(jnp.int32, (8, LANES), 0) // nb
    for q in range(S5_NQ):
        u = u_ref[:, S5_CB * q:S5_CB * (q + 1)].astype(bf16)
        x_s[...] = jnp.dot(u, wb_ref[q], preferred_element_type=f32)
        ar = [jnp.broadcast_to(a_ref[q, 0:1, LANES * c:LANES * (c + 1)], (8, LANES)) for c in range(ncc)]
        ai = [jnp.broadcast_to(a_ref[q, 1:2, LANES * c:LANES * (c + 1)], (8, LANES)) for c in range(ncc)]

        def tile_step(i, carry):
            rt = (ntile - 1 - i) if reverse else i
            r0 = pl.multiple_of(rt * 8, 8)
            new = []
            for c in range(ncc):
                cr, ci = carry[2 * c], carry[2 * c + 1]
                re_sl = slice(LANES * c, LANES * (c + 1))
                im_sl = slice(S5_BS + LANES * c, S5_BS + LANES * (c + 1))
                br = x_s[pl.ds(r0, 8), re_sl]
                bi = x_s[pl.ds(r0, 8), im_sl]
                out_r = out_i = None
                order = range(nsub - 1, -1, -1) if reverse else range(nsub)
                for s in order:
                    pr = pltpu.roll(cr, shift, 0)
                    pi = pltpu.roll(ci, shift, 0)
                    cr = ar[c] * pr - ai[c] * pi + br
                    ci = ar[c] * pi + ai[c] * pr + bi
                    out_r = cr if out_r is None else jnp.where(grp == s, cr, out_r)
                    out_i = ci if out_i is None else jnp.where(grp == s, ci, out_i)
                x_s[pl.ds(r0, 8), re_sl] = out_r
                x_s[pl.ds(r0, 8), im_sl] = out_i
                new += [out_r, out_i]
            return tuple(new)

        init = tuple(st_s[q, cc] for cc in range(2 * ncc))
        fin = lax.fori_loop(0, ntile, tile_step, init)
        for cc in range(2 * ncc):
            st_s[q, cc] = fin[cc]
        y_ref[:, S5_CB * q:S5_CB * (q + 1)] = jnp.dot(x_s[...].astype(bf16), wc_ref[q],
                                                      preferred_element_type=f32)


def _s5_scan(u_tb, wb, wc, a, nb, tpb, reverse):
    rows = TM * nb
    if reverse:
        imap = lambda c: (jnp.where(c == 0, 0, tpb - c), 0)
    else:
        imap = lambda c: (c, 0)
    full = lambda shape: pl.BlockSpec(shape, lambda c: (0,) * len(shape))
    return pl.pallas_call(
        functools.partial(_s5_kernel, nb=nb, reverse=reverse),
        grid=(tpb,),
        in_specs=[pl.BlockSpec((rows, BRANCH_W), imap),
                  full((S5_NQ, S5_CB, 2 * S5_BS)), full((S5_NQ, 2 * S5_BS, S5_CB)), full((S5_NQ, 2, S5_BS))],
        out_specs=pl.BlockSpec((rows, BRANCH_W), imap),
        out_shape=jax.ShapeDtypeStruct(u_tb.shape, f32),
        scratch_shapes=[pltpu.VMEM((rows, 2 * S5_BS), f32),
                        pltpu.VMEM((S5_NQ, 2 * S5_BS // LANES, 8, LANES), f32)],
        compiler_params=_cparams(("arbitrary",)),
        name="s5_bwd" if reverse else "s5_fwd",
    )(u_tb, wb, wc, a)


def _s5_params(lam_re, lam_im, log_dt, b_re, b_im, c_re, c_im):
    dt = jnp.exp(log_dt)[:, None]
    mag = jnp.exp(lam_re * dt)
    ar = mag * jnp.cos(lam_im * dt)
    ai = mag * jnp.sin(lam_im * dt)
    den = lam_re * lam_re + lam_im * lam_im
    cr = ((ar - 1.0) * lam_re + ai * lam_im) / den
    ci = (ai * lam_re - (ar - 1.0) * lam_im) / den
    bbr = cr[..., None] * b_re - ci[..., None] * b_im
    bbi = cr[..., None] * b_im + ci[..., None] * b_re
    gl = S5_CB // S5_GROUP
    eye = jnp.eye(gl, dtype=f32)

    def wb_half(bb):
        x = bb.reshape(S5_NQ, gl, S5_STATE, S5_GROUP)
        x = jnp.einsum('qgpc,gh->qgchp', x, eye)
        return x.reshape(S5_NQ, S5_CB, S5_BS)

    def wc_half(cc):
        x = cc.reshape(S5_NQ, gl, S5_GROUP, S5_STATE)
        x = jnp.einsum('qgcp,gh->qhpgc', x, eye)
        return x.reshape(S5_NQ, S5_BS, S5_CB)

    wb = jnp.concatenate([wb_half(bbr), wb_half(bbi)], axis=2).astype(bf16)
    wc = jnp.concatenate([wc_half(c_re), wc_half(-c_im)], axis=1).astype(bf16)
    a = jnp.stack([ar.reshape(S5_NQ, S5_BS), ai.reshape(S5_NQ, S5_BS)], axis=1)
    return wb, wc, a


def _gelu_tanh(x):
    return 0.5 * x * (1.0 + jnp.tanh(math.sqrt(2.0 / math.pi) * (x + 0.044715 * x * x * x)))


def _merge_kernel(x_ref, mod_ref, g2n_ref, yr_ref, bonus_ref, gg_ref, na_ref, y5f_ref, y5b_ref, u_ref,
                  ga_ref, gn_ref, gs_ref, lnv_ref, e_ref, gluw_ref, wbr_ref, wout_ref, rw_ref, rb_ref,
                  xo_ref, h2_ref, idx_ref, tw_ref):
    mod = mod_ref[0]
    lnv = lnv_ref[...]
    e = e_ref[...]
    y = yr_ref[...]
    mu = _segsum(y, e) * (1.0 / HEAD)
    dlt = y - mu
    var = _segsum(dlt * dlt, e) * (1.0 / HEAD)
    a_out = (dlt * lax.rsqrt(var + RWKV_GN_EPS) * lnv[0:1] + lnv[1:2] + bonus_ref[...]) * gg_ref[...]
    ys = _gelu_tanh(y5f_ref[...] + y5b_ref[...] + u_ref[...] * lnv[2:3])
    glu = jnp.dot(ys.astype(bf16), gluw_ref[...], preferred_element_type=f32) + lnv[3:4]
    s_out = ys * _sigmoid(glu)
    mix = None
    for o, gate_ref, jdx in ((a_out, ga_ref, 0), (na_ref[...], gn_ref, 1), (s_out, gs_ref, 2)):
        t = _sigmoid(gate_ref[...]) * jnp.dot(o.astype(bf16), wbr_ref[jdx], preferred_element_type=f32)
        mix = t if mix is None else mix + t
    ol = jnp.dot(mix.astype(bf16), wout_ref[...], preferred_element_type=f32)
    xn = x_ref[...] + mod[2:3] * ol
    xo_ref[...] = xn
    h2 = _rms_mod(xn, g2n_ref[...], mod[4:5], mod[3:4])
    h2_ref[...] = h2
    hi = h2.astype(bf16)
    lo = (h2 - hi.astype(f32)).astype(bf16)
    logits = (jnp.dot(hi, rw_ref[0], preferred_element_type=f32) + jnp.dot(lo, rw_ref[0], preferred_element_type=f32)
              + jnp.dot(hi, rw_ref[1], preferred_element_type=f32) + rb_ref[...])
    lane = lax.broadcasted_iota(jnp.int32, logits.shape, 1)
    logits = jnp.where(lane < N_EXPERTS, logits, -jnp.inf)
    idx_acc = jnp.zeros(logits.shape, jnp.int32)
    val_acc = jnp.full(logits.shape, -jnp.inf, f32)
    top = None
    for j in range(TOP_K):
        mx = logits.max(axis=-1, keepdims=True)
        sel = jnp.min(jnp.where(logits == mx, lane, LANES), axis=-1, keepdims=True)
        idx_acc = jnp.where(lane == j, sel, idx_acc)
        val_acc = jnp.where(lane == j, mx, val_acc)
        logits = jnp.where(lane == sel, -jnp.inf, logits)
        top = mx if top is None else top
    ex = jnp.exp(val_acc - top)
    idx_ref[...] = idx_acc
    tw_ref[...] = ex / ex.sum(axis=-1, keepdims=True)


def _merge(x2, modt, g2n, yr, bonus, gg, na, y5f, y5b, p, lnv, e, gluw, wbr, wout, rw, rb, tpb):
    m, d = x2.shape
    bw = BRANCH_W
    tok = lambda w: pl.BlockSpec((TM, w), lambda i: (i, 0))
    full = lambda shape: pl.BlockSpec(shape, lambda i: (0,) * len(shape))
    gate = lambda jdx: pl.BlockSpec((TM, d), lambda i: (i, CB_GATE + jdx))
    return pl.pallas_call(
        _merge_kernel,
        grid=(m // TM,),
        in_specs=[tok(d), pl.BlockSpec((1, 8, d), lambda i: (_seg_of_tile(i, tpb), 0, 0)), full((1, d)),
                  tok(bw), tok(bw), tok(bw), tok(bw), tok(bw), tok(bw),
                  pl.BlockSpec((TM, bw), lambda i: (i, CB_U)), gate(0), gate(1), gate(2),
                  full((8, bw)), full((bw, bw)), full((bw, bw)), full((3, bw, d)), full((d, d)),
                  full((2, d, LANES)), full((1, LANES))],
        out_specs=[tok(d), tok(d), tok(LANES), tok(LANES)],
        out_shape=[jax.ShapeDtypeStruct((m, d), f32), jax.ShapeDtypeStruct((m, d), f32),
                   jax.ShapeDtypeStruct((m, LANES), jnp.int32), jax.ShapeDtypeStruct((m, LANES), f32)],
        compiler_params=_cparams(("parallel",)),
        name="merge_router",
    )(x2, modt, g2n.reshape(1, d), yr, bonus, gg, na, y5f, y5b, p, p, p, p, lnv, e, gluw, wbr, wout, rw, rb)


def _start_row_gather(idx_vmem, idx_smem, src_hbm, dst, sem_i, sem_g, n, unroll=8):
    cp = pltpu.make_async_copy(idx_vmem, idx_smem, sem_i)
    cp.start()
    cp.wait()

    def issue(r, carry):
        pltpu.make_async_copy(src_hbm.at[idx_smem[0, r]], dst.at[r], sem_g).start()
        return carry
    lax.fori_loop(0, n, issue, 0, unroll=unroll)


def _wait_row_gather(src_hbm, dst, sem_g, n):
    pltpu.make_async_copy(src_hbm.at[pl.ds(0, n)], dst, sem_g).wait()


def _expert_kernel(be_ref, nv_ref, tok_ref, tokn_ref, h_hbm, guw_ref, gub_ref, dnw_ref, dnb_ref, y_ref,
                   idx_smem, xb, sem_i, sem_g):
    i = pl.program_id(0)
    nv = nv_ref[0]
    slot = i % 2

    @pl.when(jnp.logical_and(i == 0, nv > 0))
    def _():
        _start_row_gather(tok_ref.at[0], idx_smem, h_hbm, xb.at[0], sem_i, sem_g.at[0], MOE_BM)

    def block(prefetch):
        _wait_row_gather(h_hbm, xb.at[slot], sem_g.at[slot], MOE_BM)
        if prefetch:
            _start_row_gather(tokn_ref.at[0], idx_smem, h_hbm, xb.at[1 - slot], sem_i, sem_g.at[1 - slot],
                              MOE_BM, unroll=True)
        x = xb[slot].astype(bf16)
        gu = jnp.dot(x, guw_ref[0], preferred_element_type=f32) + gub_ref[0]
        glu = jnp.minimum(gu[:, :D_MODEL], SWIGLU_LIMIT)
        lin = jnp.clip(gu[:, D_MODEL:], -SWIGLU_LIMIT, SWIGLU_LIMIT)
        act = glu * _sigmoid(SWIGLU_ALPHA * glu) * (lin + 1.0)
        y_ref[...] = jnp.dot(act.astype(bf16), dnw_ref[0], preferred_element_type=f32) + dnb_ref[0]

    pl.when(i + 1 < nv)(functools.partial(block, True))
    pl.when(i + 1 == nv)(functools.partial(block, False))

    @pl.when(i >= nv_ref[0])
    def _():
        y_ref[...] = jnp.zeros_like(y_ref)


def _experts(block_expert, nvalid, slot_tok, h_rows, guw, gub, dnw, dnb):
    n_blocks = block_expert.shape[0]
    d = D_MODEL
    grid_spec = pltpu.PrefetchScalarGridSpec(
        num_scalar_prefetch=2,
        grid=(n_blocks,),
        in_specs=[pl.BlockSpec((1, 1, MOE_BM), lambda i, be, nv: (i, 0, 0)),
                  pl.BlockSpec((1, 1, MOE_BM), lambda i, be, nv: (jnp.minimum(i + 1, n_blocks - 1), 0, 0)),
                  pl.BlockSpec(memory_space=pl.ANY),
                  pl.BlockSpec((1, d, 2 * d), lambda i, be, nv: (be[i], 0, 0)),
                  pl.BlockSpec((1, 1, 2 * d), lambda i, be, nv: (be[i], 0, 0)),
                  pl.BlockSpec((1, d, d), lambda i, be, nv: (be[i], 0, 0)),
                  pl.BlockSpec((1, 1, d), lambda i, be, nv: (be[i], 0, 0))],
        out_specs=pl.BlockSpec((MOE_BM, d), lambda i, be, nv: (i, 0)),
        scratch_shapes=[pltpu.SMEM((1, MOE_BM), jnp.int32), pltpu.VMEM((2, MOE_BM, d), f32),
                        pltpu.SemaphoreType.DMA, pltpu.SemaphoreType.DMA((2,))],
    )
    tok3 = slot_tok.reshape(n_blocks, 1, MOE_BM)
    return pl.pallas_call(
        _expert_kernel,
        grid_spec=grid_spec,
        out_shape=jax.ShapeDtypeStruct((n_blocks * MOE_BM, d), f32),
        compiler_params=_cparams(("arbitrary",), disable_bounds_checks=True),
        name="moe_experts",
    )(block_expert, nvalid, tok3, tok3, h_rows, guw,
      gub.reshape(N_EXPERTS, 1, 2 * d), dnw, dnb.reshape(N_EXPERTS, 1, d))


def _combine_kernel(dest_ref, x_ref, mod_ref, tw_ref, fg_ref, yb_hbm, o_ref, idx_smem, buf, sem_i, sem_g, *, final):
    _start_row_gather(dest_ref.at[0], idx_smem, yb_hbm, buf, sem_i, sem_g, TOP_K * TM)
    _wait_row_gather(yb_hbm, buf, sem_g, TOP_K * TM)
    tw = tw_ref[...]
    y = None
    for j in range(TOP_K):
        t = tw[:, j:j + 1] * buf[TM * j:TM * (j + 1), :]
        y = t if y is None else y + t
    xn = x_ref[...] + mod_ref[0][5:6] * y
    if final:
        xn = xn * lax.rsqrt(jnp.mean(xn * xn, axis=-1, keepdims=True) + NORM_EPS) * fg_ref[...]
    o_ref[...] = xn


def _combine(dest_t, x2, modt, tw, fg, yb, tpb, final):
    m, d = x2.shape
    return pl.pallas_call(
        functools.partial(_combine_kernel, final=final),
        grid=(m // TM,),
        in_specs=[pl.BlockSpec((1, 1, TOP_K * TM), lambda i: (i, 0, 0)),
                  pl.BlockSpec((TM, d), lambda i: (i, 0)),
                  pl.BlockSpec((1, 8, d), lambda i: (_seg_of_tile(i, tpb), 0, 0)),
                  pl.BlockSpec((TM, LANES), lambda i: (i, 0)),
                  pl.BlockSpec((1, d), lambda i: (0, 0)),
                  pl.BlockSpec(memory_space=pl.ANY)],
        out_specs=pl.BlockSpec((TM, d), lambda i: (i, 0)),
        out_shape=jax.ShapeDtypeStruct((m, d), f32),
        scratch_shapes=[pltpu.SMEM((1, TOP_K * TM), jnp.int32), pltpu.VMEM((TOP_K * TM, d), f32),
                        pltpu.SemaphoreType.DMA, pltpu.SemaphoreType.DMA],
        compiler_params=_cparams(("arbitrary",), disable_bounds_checks=True),
        name="moe_combine",
    )(dest_t, x2, modt, tw, fg.reshape(1, d), yb)


def _dispatch(idx4, m):
    n_assign = m * TOP_K
    e_flat = idx4.reshape(-1)
    onehot = (e_flat[:, None] == jnp.arange(N_EXPERTS, dtype=jnp.int32)[None, :]).astype(jnp.int32)
    csum = jnp.cumsum(onehot, axis=0)
    rank = jnp.take_along_axis(csum, e_flat[:, None], axis=1)[:, 0] - 1
    counts = csum[-1]
    padded = (counts + MOE_BM - 1) // MOE_BM * MOE_BM
    pad_end = jnp.cumsum(padded)
    pad_start = pad_end - padded
    dest = (pad_start[e_flat] + rank).astype(jnp.int32)
    n_blocks = -(-n_assign // MOE_BM) + N_EXPERTS
    slot_tok = jnp.zeros((n_blocks * MOE_BM,), jnp.int32).at[dest].set(
        jnp.arange(n_assign, dtype=jnp.int32) // TOP_K)
    block_expert = jnp.minimum(
        jnp.searchsorted(pad_end, jnp.arange(n_blocks, dtype=jnp.int32) * MOE_BM, side='right'),
        N_EXPERTS - 1).astype(jnp.int32)
    nvalid = (pad_end[-1:] // MOE_BM).astype(jnp.int32)
    return dest, slot_tok, block_expert, nvalid


def _layer(x2, modt, lp, b, tpb, final_g, last):
    m = x2.shape[0]
    ttot = m // b
    p = _proj_in(x2, lp['norm1_g'], modt, lp['w7'], tpb)

    (r, v, kk, d0, d1, b0, b1, k0, k1, gg, bonus) = _rwkv_prep(
        p, lp['mu'], lp['vec'], lp['w2p'], lp['a2p'], lp['g2'], lp['e'], tpb)
    xt = {'r': r, 'kk': kk, 'd0': d0, 'd1': d1, 'b0': b0, 'b1': b1, 'k0': k0, 'k1': k1}
    yr = _from_scan_y(*_rwkv_scan(xt, _to_scan_v(v, b), b), b)

    na = _na(p, lp['na_bias'], b, tpb)

    u = p[:, CB_U * BRANCH_W:(CB_U + 1) * BRANCH_W]
    u_tb = jnp.transpose(u.reshape(b, ttot, BRANCH_W), (1, 0, 2)).reshape(ttot * b, BRANCH_W)
    to_bt = lambda y: jnp.transpose(y.reshape(ttot, b, BRANCH_W), (1, 0, 2)).reshape(m, BRANCH_W)
    y5f = to_bt(_s5_scan(u_tb, *lp['s5'][0], b, tpb, False))
    y5b = to_bt(_s5_scan(u_tb, *lp['s5'][1], b, tpb, True))

    xn, h2, idx, tw = _merge(x2, modt, lp['norm2_g'], yr, bonus, gg, na, y5f, y5b, p, lp['lnv'], lp['e'],
                             lp['gluw'], lp['wbr'], lp['wout'], lp['rw'], lp['rb'], tpb)

    dest, slot_tok, block_expert, nvalid = _dispatch(idx[:, :TOP_K], m)
    yb = _experts(block_expert, nvalid, slot_tok, h2, lp['guw'], lp['gub'], lp['dnw'], lp['dnb'])
    dest_t = jnp.transpose(dest.reshape(m // TM, TM, TOP_K), (0, 2, 1)).reshape(m // TM, 1, TOP_K * TM)
    return _combine(dest_t, xn, modt, tw, final_g, yb, tpb, last)


def _pad_rows(w, before, total):
    return jnp.pad(w, ((0, 0), (before, total - before - w.shape[1]), (0, 0)))


def kernel(x, c, ctx, c_ctx, ada_w, ada_b, norm1_g, norm2_g, w_in, rwkv_mu_prev, rwkv_mu_next, rwkv_w0, rwkv_w2, rwkv_a0, rwkv_a2, rwkv_g2, rwkv_k_k, rwkv_k_a, rwkv_r_k, rwkv_ln_w, rwkv_ln_b, na_rpb, s5_lambda_re, s5_lambda_im, s5_log_dt, s5_b_re, s5_b_im, s5_c_re, s5_c_im, s5_d, s5_glu_w, s5_glu_b, w_branch, w_out, router_w, router_b, expert_gu_w, expert_gu_b, expert_dn_w, expert_dn_b, final_g):
    b, s, d = x.shape
    l = ctx.shape[1]
    depth = ada_w.shape[0]
    assert d == D_MODEL and l == TM and s % TM == 0 and (s // GRID_W) >= 2 * (TM // GRID_W) and 8 % b == 0
    ttot = l + s
    tpb = ttot // TM
    m = b * ttot

    c8 = jnp.zeros((8, d), f32).at[:b].set(c).at[b].set(c_ctx)
    mods = _ada(c8, ada_w, ada_b).reshape(depth, 8, 6, d)
    eye_h = jnp.kron(jnp.eye(HEADS, dtype=f32), jnp.ones((HEAD, HEAD), f32)).astype(bf16)
    zeros_bw = jnp.zeros((BRANCH_W,), f32)

    x2 = jnp.concatenate([ctx, x], axis=1).reshape(m, d)
    for i in range(depth):
        mi = mods[i]
        modt = jnp.stack([jnp.broadcast_to(mi[b], (b, 6, d)), mi[:b]], axis=1).reshape(2 * b, 6, d)
        modt = jnp.pad(modt, ((0, 0), (0, 2), (0, 0)))
        w = w_in[i]
        w7 = jnp.concatenate([w[:, :RWKV_COLS], jnp.zeros((d, COL_PAD), f32), w[:, RWKV_COLS:]], axis=1).astype(bf16)
        rw = jnp.pad(router_w[i], ((0, 0), (0, LANES - N_EXPERTS)))
        rw_hi = rw.astype(bf16)
        rw_lo = (rw - rw_hi.astype(f32)).astype(bf16)
        lp = {
            'norm1_g': norm1_g[i], 'norm2_g': norm2_g[i], 'w7': w7,
            'mu': jnp.stack([rwkv_mu_prev[i], rwkv_mu_next[i]], 0),
            'vec': jnp.stack([rwkv_k_k[i], rwkv_k_a[i], rwkv_r_k[i].reshape(-1), rwkv_w0[i, 0], rwkv_w0[i, 1],
                              rwkv_a0[i, 0], rwkv_a0[i, 1], zeros_bw], 0),
            'w2p': _pad_rows(rwkv_w2[i], 0, LANES).astype(bf16),
            'a2p': _pad_rows(rwkv_a2[i], LORA_W, LANES).astype(bf16),
            'g2': rwkv_g2[i].astype(bf16), 'e': eye_h,
            'na_bias': _na_bias_tables(na_rpb[i]),
            's5': [_s5_params(s5_lambda_re[i, dd], s5_lambda_im[i, dd], s5_log_dt[i, dd], s5_b_re[i, dd],
                              s5_b_im[i, dd], s5_c_re[i, dd], s5_c_im[i, dd]) for dd in range(2)],
            'lnv': jnp.stack([rwkv_ln_w[i], rwkv_ln_b[i], s5_d[i], s5_glu_b[i]] + [zeros_bw] * 4, 0),
            'gluw': s5_glu_w[i].astype(bf16), 'wbr': w_branch[i].astype(bf16), 'wout': w_out[i].astype(bf16),
            'rw': jnp.stack([rw_hi, rw_lo], 0), 'rb': jnp.pad(router_b[i], (0, LANES - N_EXPERTS)).reshape(1, LANES),
            'guw': expert_gu_w[i].astype(bf16), 'gub': expert_gu_b[i],
            'dnw': expert_dn_w[i].astype(bf16), 'dnb': expert_dn_b[i],
        }
        x2 = _layer(x2, modt, lp, b, tpb, final_g, i == depth - 1)
    return x2.reshape(b, ttot, d)[:, l:]
```
